```python
import jax, jax.numpy as jnp
from jax import lax
import numpy as np

D_MODEL = 2048
BATCH = 2
SEQ = 8192
DEPTH = 4

HEAD_DIM = 128
SG_WIDTH = D_MODEL // 4
SG_HEADS = SG_WIDTH // HEAD_DIM
SG_CHUNK = 128
POOL_WINDOWS = (2, 4, 8, 16)
POOL_GROUPS = len(POOL_WINDOWS)
POOL_WIDTH = D_MODEL // 4
POOL_CH = POOL_WIDTH // POOL_GROUPS
NA_WIDTH = D_MODEL // 2
NA_HEADS = NA_WIDTH // HEAD_DIM
NA_KH = 8
NA_KW = 16
GRID_W = 64
MIX_WIDTH = SG_WIDTH + POOL_WIDTH + NA_WIDTH
IN_COLS = 2 * SG_WIDTH + POOL_WIDTH + 3 * NA_WIDTH
D_FF = 11 * D_MODEL // 4
EPS = 1e-6
NEG = -1e30

kernel_name = "hybrid_gmlp_pool_natten_macaron_encoder"


def rms_norm(x, g):
    xf = x.astype(jnp.float32)
    y = xf * lax.rsqrt(jnp.mean(xf * xf, axis=-1, keepdims=True) + EPS)
    return (y * g.astype(jnp.float32)).astype(x.dtype)


def swiglu(h, w_gate, w_up, w_down):
    return (jax.nn.silu(h @ w_gate) * (h @ w_up)) @ w_down


def spatial_gating(zu, zv, g, w_s, b_s):
    B, S, _ = zu.shape
    u = jax.nn.gelu(zu, approximate=False)
    v = jax.nn.gelu(zv, approximate=False).reshape(B, S // SG_CHUNK, SG_CHUNK, SG_HEADS, HEAD_DIM)
    v = rms_norm(v, g.reshape(SG_HEADS, HEAD_DIM))
    mixed = jnp.einsum('hpq,bnqhd->bnphd', w_s, v) + b_s.T[None, None, :, :, None]
    return u * mixed.reshape(B, S, SG_WIDTH)


def multiscale_pool(p, w, scale):
    B, S, _ = p.shape
    pf = p.astype(jnp.float32).reshape(B, S, POOL_GROUPS, POOL_CH)
    cs = jnp.concatenate([jnp.zeros((B, 1, POOL_GROUPS, POOL_CH), jnp.float32),
                          jnp.cumsum(pf, axis=1)], axis=1)
    t = jnp.arange(S)
    outs = []
    for g, win in enumerate(POOL_WINDOWS):
        lo = jnp.clip(t - win // 2, 0, S)
        hi = jnp.clip(t + win // 2, 0, S)
        cnt = (hi - lo).astype(jnp.float32)
        mean = (cs[:, hi, g] - cs[:, lo, g]) / cnt[None, :, None]
        outs.append(mean - pf[:, :, g])
    d = jnp.stack(outs, axis=2).astype(p.dtype)
    y = jnp.einsum('bsgc,gcd->bsgd', d, w) * scale.reshape(POOL_GROUPS, POOL_CH)
    return y.reshape(B, S, POOL_WIDTH)


def neighbourhood_attention(q, k, v, rpb):
    B, S, H, Dh = q.shape
    rows = S // GRID_W
    kh = min(NA_KH, rows)
    qg = q.reshape(B, rows, GRID_W, H, Dh)
    kg = k.reshape(B, rows, GRID_W, H, Dh)
    vg = v.reshape(B, rows, GRID_W, H, Dh)
    col = jnp.arange(GRID_W)
    col_start = jnp.clip(col - NA_KW // 2, 0, GRID_W - NA_KW)
    col_in = (col[None, :] >= col_start[:, None]) & (col[None, :] < col_start[:, None] + NA_KW)
    dc_idx = jnp.clip(col[None, :] - col[:, None] + NA_KW - 1, 0, 2 * NA_KW - 2)
    rpb_col = rpb.astype(jnp.float32)[:, :, dc_idx]
    scale = Dh ** -0.5

    def one_row(r):
        sr = jnp.clip(r - kh // 2, 0, rows - kh)
        q_r = lax.dynamic_index_in_dim(qg, r, axis=1, keepdims=False)
        k_r = lax.dynamic_slice_in_dim(kg, sr, kh, axis=1)
        v_r = lax.dynamic_slice_in_dim(vg, sr, kh, axis=1)
        dr = sr + jnp.arange(kh) - r + NA_KH - 1
        bias = jnp.take(rpb_col, dr, axis=1).transpose(0, 2, 1, 3)
        s = jnp.einsum('bqhd,bjkhd->bhqjk', q_r, k_r).astype(jnp.float32) * scale + bias[None]
        s = jnp.where(col_in[:, None, :], s, NEG)
        pr = jax.nn.softmax(s.reshape(B, H, GRID_W, kh * GRID_W), axis=-1)
        pr = pr.reshape(B, H, GRID_W, kh, GRID_W).astype(v.dtype)
        return jnp.einsum('bhqjk,bjkhd->bqhd', pr, v_r)

    out = lax.map(one_row, jnp.arange(rows))
    return out.transpose(1, 0, 2, 3, 4).reshape(B, S, H, Dh)


def setup_inputs(seed: int = 0) -> dict:
    key = jax.random.key(seed)
    ks = jax.random.split(key, 20)
    f32 = jnp.float32
    nrm = lambda k, shape, s: jax.random.normal(k, shape, f32) * s
    L = DEPTH
    return {
        "x": jax.random.normal(ks[0], (BATCH, SEQ, D_MODEL), f32),
        "ffn1_norm": 1.0 + nrm(ks[1], (L, D_MODEL), 0.05),
        "ffn1_w_gate": nrm(ks[2], (L, D_MODEL, D_FF), D_MODEL ** -0.5),
        "ffn1_w_up": nrm(ks[3], (L, D_MODEL, D_FF), D_MODEL ** -0.5),
        "ffn1_w_down": nrm(ks[4], (L, D_FF, D_MODEL), D_FF ** -0.5),
        "mix_norm": 1.0 + nrm(ks[5], (L, D_MODEL), 0.05),
        "w_in": nrm(ks[6], (L, D_MODEL, IN_COLS), D_MODEL ** -0.5),
        "sg_norm": 1.0 + nrm(ks[7], (L, SG_WIDTH), 0.05),
        "sg_w": nrm(ks[8], (L, SG_HEADS, SG_CHUNK, SG_CHUNK), SG_CHUNK ** -0.5),
        "sg_b": 1.0 + nrm(ks[9], (L, SG_HEADS, SG_CHUNK), 0.05),
        "pool_w": nrm(ks[10], (L, POOL_GROUPS, POOL_CH, POOL_CH), POOL_CH ** -0.5),
        "pool_scale": 1.0 + nrm(ks[11], (L, POOL_WIDTH), 0.1),
        "na_rpb": nrm(ks[12], (L, NA_HEADS, 2 * NA_KH - 1, 2 * NA_KW - 1), 0.1),
        "w_out": nrm(ks[13], (L, MIX_WIDTH, D_MODEL), MIX_WIDTH ** -0.5),
        "ffn2_norm": 1.0 + nrm(ks[14], (L, D_MODEL), 0.05),
        "ffn2_w_gate": nrm(ks[15], (L, D_MODEL, D_FF), D_MODEL ** -0.5),
        "ffn2_w_up": nrm(ks[16], (L, D_MODEL, D_FF), D_MODEL ** -0.5),
        "ffn2_w_down": nrm(ks[17], (L, D_FF, D_MODEL), D_FF ** -0.5),
        "final_norm": 1.0 + nrm(ks[18], (D_MODEL,), 0.05),
    }


def reference(x, ffn1_norm, ffn1_w_gate, ffn1_w_up, ffn1_w_down, mix_norm, w_in,
              sg_norm, sg_w, sg_b, pool_w, pool_scale, na_rpb, w_out,
              ffn2_norm, ffn2_w_gate, ffn2_w_up, ffn2_w_down, final_norm):
    B, S, _ = x.shape
    splits = [SG_WIDTH, 2 * SG_WIDTH, 2 * SG_WIDTH + POOL_WIDTH,
              2 * SG_WIDTH + POOL_WIDTH + NA_WIDTH, 2 * SG_WIDTH + POOL_WIDTH + 2 * NA_WIDTH]
    for l in range(DEPTH):
        x = x + 0.5 * swiglu(rms_norm(x, ffn1_norm[l]), ffn1_w_gate[l], ffn1_w_up[l], ffn1_w_down[l])
        h = rms_norm(x, mix_norm[l])
        z = h @ w_in[l]
        zu, zv, zp, zq, zk, zvv = jnp.split(z, splits, axis=-1)
        a = spatial_gating(zu, zv, sg_norm[l], sg_w[l], sg_b[l])
        bp = multiscale_pool(zp, pool_w[l], pool_scale[l])
        c = neighbourhood_attention(zq.reshape(B, S, NA_HEADS, HEAD_DIM),
                                    zk.reshape(B, S, NA_HEADS, HEAD_DIM),
                                    zvv.reshape(B, S, NA_HEADS, HEAD_DIM),
                                    na_rpb[l]).reshape(B, S, NA_WIDTH)
        x = x + jnp.concatenate([a, bp, c], axis=-1) @ w_out[l]
        x = x + 0.5 * swiglu(rms_norm(x, ffn2_norm[l]), ffn2_w_gate[l], ffn2_w_up[l], ffn2_w_down[l])
    return rms_norm(x, final_norm)
```

```python
import functools

import jax
import jax.numpy as jnp
import numpy as np
from jax import lax
from jax.experimental import pallas as pl
from jax.experimental.pallas import tpu as pltpu

EPS = 1e-6
NEG = -1e30
HEAD_DIM = 128
SG_CHUNK = 128
POOL_WINDOWS = (2, 4, 8, 16)
POOL_HALO = 8
NA_KH = 8
NA_KW = 16
GRID_W = 64
NA_ROWS_PER_BLOCK = 4
NA_WIN_ROWS = 12

V7X_VMEM_LIMIT_BYTES = 56 * 1024 * 1024

F32 = jnp.float32
BF16 = jnp.bfloat16


def _rms_norm(x, g):
    return x * lax.rsqrt(jnp.mean(x * x, axis=-1, keepdims=True) + EPS) * g


def _gelu(x):
    return 0.5 * x * (1.0 + lax.erf(x * np.float32(np.sqrt(0.5))))


def _ffn_kernel(x_ref, g_ref, wg_ref, wu_ref, wd_ref, gf_ref, o_ref, h_ref, *, final_norm):
    j = pl.program_id(1)

    @pl.when(j == 0)
    def _():
        h_ref[...] = _rms_norm(x_ref[...], g_ref[...]).astype(BF16)
        o_ref[...] = jnp.zeros_like(o_ref)

    h = h_ref[...]
    gate = jnp.dot(h, wg_ref[...], preferred_element_type=F32)
    up = jnp.dot(h, wu_ref[...], preferred_element_type=F32)
    act = (jax.nn.silu(gate) * up).astype(BF16)
    o_ref[...] += jnp.dot(act, wd_ref[...], preferred_element_type=F32)

    @pl.when(j == pl.num_programs(1) - 1)
    def _():
        y = x_ref[...] + 0.5 * o_ref[...]
        if final_norm:
            y = _rms_norm(y, gf_ref[...])
        o_ref[...] = y


def _ffn(x, g, wg, wu, wd, gf, *, final_norm, tm, tf):
    T, D = x.shape
    F = wg.shape[1]
    return pl.pallas_call(
        functools.partial(_ffn_kernel, final_norm=final_norm),
        grid=(T // tm, F // tf),
        in_specs=[
            pl.BlockSpec((tm, D), lambda i, j: (i, 0)),
            pl.BlockSpec((1, D), lambda i, j: (0, 0)),
            pl.BlockSpec((D, tf), lambda i, j: (0, j)),
            pl.BlockSpec((D, tf), lambda i, j: (0, j)),
            pl.BlockSpec((tf, D), lambda i, j: (j, 0)),
            pl.BlockSpec((1, D), lambda i, j: (0, 0)),
        ],
        out_specs=pl.BlockSpec((tm, D), lambda i, j: (i, 0)),
        out_shape=jax.ShapeDtypeStruct((T, D), F32),
        scratch_shapes=[pltpu.VMEM((tm, D), BF16)],
        compiler_params=pltpu.CompilerParams(
            dimension_semantics=("parallel", "arbitrary"),
            vmem_limit_bytes=V7X_VMEM_LIMIT_BYTES),
        name="ffn",
    )(x, g, wg, wu, wd, gf)


def _inproj_kernel(x_ref, g_ref, w_ref, uvp_ref, qkv_ref, h_ref, *, n_f32_blocks):
    j = pl.program_id(1)

    @pl.when(j == 0)
    def _():
        h_ref[...] = _rms_norm(x_ref[...], g_ref[...]).astype(BF16)

    z = jnp.dot(h_ref[...], w_ref[...], preferred_element_type=F32)

    @pl.when(j < n_f32_blocks)
    def _():
        uvp_ref[...] = z

    @pl.when(j >= n_f32_blocks)
    def _():
        qkv_ref[...] = z.astype(BF16)


def _inproj(x, g, w, *, n_f32_cols, tm, tn):
    T, D = x.shape
    N = w.shape[1]
    nf = n_f32_cols // tn
    return pl.pallas_call(
        functools.partial(_inproj_kernel, n_f32_blocks=nf),
        grid=(T // tm, N // tn),
        in_specs=[
            pl.BlockSpec((tm, D), lambda i, j: (i, 0)),
            pl.BlockSpec((1, D), lambda i, j: (0, 0)),
            pl.BlockSpec((D, tn), lambda i, j: (0, j)),
        ],
        out_specs=[
            pl.BlockSpec((tm, tn), lambda i, j: (i, jnp.minimum(j, nf - 1))),
            pl.BlockSpec((tm, tn), lambda i, j: (i, jnp.maximum(j - nf, 0))),
        ],
        out_shape=[
            jax.ShapeDtypeStruct((T, n_f32_cols), F32),
            jax.ShapeDtypeStruct((T, N - n_f32_cols), BF16),
        ],
        scratch_shapes=[pltpu.VMEM((tm, D), BF16)],
        compiler_params=pltpu.CompilerParams(
            dimension_semantics=("parallel", "arbitrary"),
            vmem_limit_bytes=V7X_VMEM_LIMIT_BYTES),
        name="inproj",
    )(x, g, w)


def _gate_pool_kernel(zu_ref, zv_ref, zp_ref, prev_ref, next_ref, sgn_ref, ws_ref, bs_ref,
                      pw_ref, ps_ref, o_ref, ext_ref, *, seq_len):
    tm, sg_width = zu_ref.shape
    pool_width = zp_ref.shape[1]
    n_heads = sg_width // HEAD_DIM
    n_groups = pool_width // HEAD_DIM

    for c in range(tm // SG_CHUNK):
        rows = slice(c * SG_CHUNK, (c + 1) * SG_CHUNK)
        for h in range(n_heads):
            cols = slice(h * HEAD_DIM, (h + 1) * HEAD_DIM)
            u = _gelu(zu_ref[rows, cols])
            v = _rms_norm(_gelu(zv_ref[rows, cols]), sgn_ref[:, cols])
            mixed = jnp.dot(ws_ref[h], v.astype(BF16), preferred_element_type=F32) + bs_ref[h]
            o_ref[rows, cols] = (u * mixed).astype(o_ref.dtype)

    tile_pos = (pl.program_id(0) * tm) % seq_len
    is_first = tile_pos == 0
    is_last = tile_pos + tm == seq_len
    ext_ref[0:POOL_HALO, :] = jnp.where(is_first, 0.0, prev_ref[...])
    ext_ref[POOL_HALO:POOL_HALO + tm, :] = zp_ref[...]
    ext_ref[POOL_HALO + tm:, :] = jnp.where(is_last, 0.0, next_ref[...])
    pos = tile_pos + lax.broadcasted_iota(jnp.int32, (tm, HEAD_DIM), 0)
    for g in range(n_groups):
        cols = slice(g * HEAD_DIM, (g + 1) * HEAD_DIM)
        half = POOL_WINDOWS[g] // 2
        total = ext_ref[POOL_HALO - half:POOL_HALO - half + tm, cols]
        for k in range(1 - half, half):
            total = total + ext_ref[POOL_HALO + k:POOL_HALO + k + tm, cols]
        cnt = jnp.minimum(pos + half, seq_len) - jnp.maximum(pos - half, 0)
        d = total / cnt.astype(F32) - zp_ref[:, cols]
        y = jnp.dot(d.astype(BF16), pw_ref[g], preferred_element_type=F32) * ps_ref[:, cols]
        o_ref[:, sg_width + g * HEAD_DIM:sg_width + (g + 1) * HEAD_DIM] = y.astype(o_ref.dtype)


def _gate_pool(uvp, sgn, ws, bs, pw, ps, *, seq_len, sg_width, pool_width, tm):
    T = uvp.shape[0]
    assert sg_width == pool_width and seq_len % tm == 0 and tm % SG_CHUNK == 0
    halo_blocks = tm // POOL_HALO
    n_heads = sg_width // HEAD_DIM
    n_groups = pool_width // HEAD_DIM
    return pl.pallas_call(
        functools.partial(_gate_pool_kernel, seq_len=seq_len),
        grid=(T // tm,),
        in_specs=[
            pl.BlockSpec((tm, sg_width), lambda i: (i, 0)),
            pl.BlockSpec((tm, sg_width), lambda i: (i, 1)),
            pl.BlockSpec((tm, pool_width), lambda i: (i, 2)),
            pl.BlockSpec((POOL_HALO, pool_width), lambda i: (jnp.maximum(i * halo_blocks - 1, 0), 2)),
            pl.BlockSpec((POOL_HALO, pool_width),
                         lambda i: (jnp.minimum((i + 1) * halo_blocks, T // POOL_HALO - 1), 2)),
            pl.BlockSpec((1, sg_width), lambda i: (0, 0)),
            pl.BlockSpec((n_heads, SG_CHUNK, SG_CHUNK), lambda i: (0, 0, 0)),
            pl.BlockSpec((n_heads, SG_CHUNK, HEAD_DIM), lambda i: (0, 0, 0)),
            pl.BlockSpec((n_groups, HEAD_DIM, HEAD_DIM), lambda i: (0, 0, 0)),
            pl.BlockSpec((1, pool_width), lambda i: (0, 0)),
        ],
        out_specs=pl.BlockSpec((tm, sg_width + pool_width), lambda i: (i, 0)),
        out_shape=jax.ShapeDtypeStruct((T, sg_width + pool_width), BF16),
        scratch_shapes=[pltpu.VMEM((tm + 2 * POOL_HALO, pool_width), F32)],
        compiler_params=pltpu.CompilerParams(
            dimension_semantics=("parallel",),
            vmem_limit_bytes=V7X_VMEM_LIMIT_BYTES),
        name="gate_pool",
    )(uvp, uvp, uvp, uvp, uvp, sgn, ws, bs, pw, ps)


def _attn_kernel(q_ref, k_ref, v_ref, b_ref, o_ref, *, n_rows):
    r0 = pl.program_id(2) * NA_ROWS_PER_BLOCK
    win_start = jnp.clip(r0 - NA_KH // 2, 0, n_rows - NA_WIN_ROWS)
    start = pl.multiple_of(win_start * GRID_W, NA_ROWS_PER_BLOCK * GRID_W)
    q = q_ref[0]
    k = k_ref[0, pl.ds(start, NA_WIN_ROWS * GRID_W), :]
    v = v_ref[0, pl.ds(start, NA_WIN_ROWS * GRID_W), :]
    s = lax.dot_general(q, k, (((1,), (1,)), ((), ())), preferred_element_type=F32)
    s = s * np.float32(HEAD_DIM ** -0.5) + b_ref[0, 0]
    p = jnp.exp(s - jnp.max(s, axis=-1, keepdims=True))
    denom = jnp.sum(p, axis=-1, keepdims=True)
    o = jnp.dot(p.astype(BF16), v, preferred_element_type=F32) / denom
    o_ref[0] = o.astype(o_ref.dtype)


def _attn_bias_table(rpb, n_rows):
    R, W = NA_ROWS_PER_BLOCK, NA_WIN_ROWS
    kh = min(NA_KH, n_rows)
    tables = []
    for r0 in (0, R, n_rows - R):
        win_start = int(np.clip(r0 - NA_KH // 2, 0, n_rows - W))
        r = r0 + np.arange(R)[:, None, None, None]
        cq = np.arange(GRID_W)[None, :, None, None]
        row = win_start + np.arange(W)[None, None, :, None]
        ck = np.arange(GRID_W)[None, None, None, :]
        sr = np.clip(r - kh // 2, 0, n_rows - kh)
        cs = np.clip(cq - NA_KW // 2, 0, GRID_W - NA_KW)
        valid = (row >= sr) & (row < sr + kh) & (ck >= cs) & (ck < cs + NA_KW)
        dr = np.clip(row - r + NA_KH - 1, 0, 2 * NA_KH - 2)
        dc = np.clip(ck - cq + NA_KW - 1, 0, 2 * NA_KW - 2)
        shape = (R, GRID_W, W, GRID_W)
        valid, dr, dc = (np.broadcast_to(a, shape).reshape(R * GRID_W, W * GRID_W) for a in (valid, dr, dc))
        tables.append(jnp.where(valid[None], rpb[:, dr, dc], NEG))
    return jnp.stack(tables, axis=0)


def _attn(qkv, bias, *, n_heads, seq_len):
    B = qkv.shape[0]
    n_rows = seq_len // GRID_W
    R, W = NA_ROWS_PER_BLOCK, NA_WIN_ROWS
    assert n_rows % R == 0 and n_rows >= W and n_rows // R >= 2
    n_blocks = n_rows // R

    def bias_map(b, h, i):
        kind = jnp.where(i == 0, 0, jnp.where(i == n_blocks - 1, 2, 1))
        return (kind, h, 0, 0)

    return pl.pallas_call(
        functools.partial(_attn_kernel, n_rows=n_rows),
        grid=(B, n_heads, n_blocks),
        in_specs=[
            pl.BlockSpec((1, R * GRID_W, HEAD_DIM), lambda b, h, i: (b, i, h)),
            pl.BlockSpec((1, seq_len, HEAD_DIM), lambda b, h, i: (b, 0, n_heads + h)),
            pl.BlockSpec((1, seq_len, HEAD_DIM), lambda b, h, i: (b, 0, 2 * n_heads + h)),
            pl.BlockSpec((1, 1, R * GRID_W, W * GRID_W), bias_map),
        ],
        out_specs=pl.BlockSpec((1, R * GRID_W, HEAD_DIM), lambda b, h, i: (b, i, h)),
        out_shape=jax.ShapeDtypeStruct((B, seq_len, n_heads * HEAD_DIM), BF16),
        compiler_params=pltpu.CompilerParams(
            dimension_semantics=("parallel", "parallel", "arbitrary"),
            vmem_limit_bytes=V7X_VMEM_LIMIT_BYTES),
        name="nbr_attn",
    )(qkv, qkv, qkv, bias)


def _outproj_kernel(x_ref, ab_ref, c_ref, w1_ref, w2_ref, o_ref):
    y = jnp.dot(ab_ref[...], w1_ref[...], preferred_element_type=F32)
    y = y + jnp.dot(c_ref[...], w2_ref[...], preferred_element_type=F32)
    o_ref[...] = x_ref[...] + y


def _outproj(x, ab, c, w, *, tm):
    T, D = x.shape
    K1, K2 = ab.shape[1], c.shape[1]
    assert K1 == K2 and w.shape[0] == K1 + K2
    return pl.pallas_call(
        _outproj_kernel,
        grid=(T // tm,),
        in_specs=[
            pl.BlockSpec((tm, D), lambda i: (i, 0)),
            pl.BlockSpec((tm, K1), lambda i: (i, 0)),
            pl.BlockSpec((tm, K2), lambda i: (i, 0)),
            pl.BlockSpec((K1, D), lambda i: (0, 0)),
            pl.BlockSpec((K2, D), lambda i: (1, 0)),
        ],
        out_specs=pl.BlockSpec((tm, D), lambda i: (i, 0)),
        out_shape=jax.ShapeDtypeStruct((T, D), F32),
        compiler_params=pltpu.CompilerParams(
            dimension_semantics=("parallel",),
            vmem_limit_bytes=V7X_VMEM_LIMIT_BYTES),
        name="outproj",
    )(x, ab, c, w, w)


def _pick_tile(n, target):
    t = min(n, target)
    while n % t:
        t //= 2
    return t


def kernel(x, ffn1_norm, ffn1_w_gate, ffn1_w_up, ffn1_w_down, mix_norm, w_in, sg_norm, sg_w, sg_b, pool_w, pool_scale, na_rpb, w_out, ffn2_norm, ffn2_w_gate, ffn2_w_up, ffn2_w_down, final_norm):
    B, S, D = x.shape
    depth = ffn1_w_gate.shape[0]
    sg_width = sg_norm.shape[1]
    pool_width = pool_scale.shape[1]
    n_attn_heads = na_rpb.shape[1]
    T = B * S
    tm_ffn = _pick_tile(T, 1024)
    tf = _pick_tile(ffn1_w_gate.shape[2], 256)
    tm_mix = _pick_tile(S, 512)

    bf = lambda w: w.astype(BF16)
    w1g, w1u, w1d = bf(ffn1_w_gate), bf(ffn1_w_up), bf(ffn1_w_down)
    w2g, w2u, w2d = bf(ffn2_w_gate), bf(ffn2_w_up), bf(ffn2_w_down)
    win, wout, sgw, pw = bf(w_in), bf(w_out), bf(sg_w), bf(pool_w)
    sgb = jnp.broadcast_to(sg_b[..., None], sg_b.shape + (HEAD_DIM,))
    gf = final_norm.reshape(1, D)

    h = x.reshape(T, D)
    for l in range(depth):
        h = _ffn(h, ffn1_norm[l].reshape(1, D), w1g[l], w1u[l], w1d[l], gf,
                 final_norm=False, tm=tm_ffn, tf=tf)
        uvp, qkv = _inproj(h, mix_norm[l].reshape(1, D), win[l],
                           n_f32_cols=2 * sg_width + pool_width, tm=tm_ffn, tn=512)
        ab = _gate_pool(uvp, sg_norm[l].reshape(1, sg_width), sgw[l], sgb[l], pw[l],
                        pool_scale[l].reshape(1, pool_width),
                        seq_len=S, sg_width=sg_width, pool_width=pool_width, tm=tm_mix)
        bias = _attn_bias_table(na_rpb[l], S // GRID_W)
        c = _attn(qkv.reshape(B, S, -1), bias, n_heads=n_attn_heads, seq_len=S)
        h = _outproj(h, ab, c.reshape(T, -1), wout[l], tm=tm_mix)
        h = _ffn(h, ffn2_norm[l].reshape(1, D), w2g[l], w2u[l], w2d[l], gf,
                 final_norm=(l == depth - 1), tm=tm_ffn, tf=tf)
    return h.reshape(B, S, D)
```

```python
import functools

import jax
import jax.numpy as jnp
import numpy as np
from jax import lax
from jax.experimental import pallas as pl
from jax.experimental.pallas import tpu as pltpu

EPS = 1e-6
NEG = -1e30
HEAD_DIM = 128
SG_CHUNK = 128
POOL_WINDOWS = (2, 4, 8, 16)
POOL_HALO = 8
NA_KH = 8
NA_KW = 16
GRID_W = 64
NA_ROWS_PER_BLOCK = 4
NA_WIN_ROWS = 12

V7X_VMEM_LIMIT_BYTES = 56 * 1024 * 1024

F32 = jnp.float32
BF16 = jnp.bfloat16


def _rms_norm(x, g):
    return x * lax.rsqrt(jnp.mean(x * x, axis=-1, keepdims=True) + EPS) * g


def _gelu(x):
    return 0.5 * x * (1.0 + lax.erf(x * np.float32(np.sqrt(0.5))))


def _ffn_kernel(x_ref, g_ref, wg_ref, wu_ref, wd_ref, gf_ref, o_ref, h_ref, *, final_norm):
    j = pl.program_id(1)

    @pl.when(j == 0)
    def _():
        h_ref[...] = _rms_norm(x_ref[...], g_ref[...]).astype(BF16)
        o_ref[...] = jnp.zeros_like(o_ref)

    h = h_ref[...]
    gate = jnp.dot(h, wg_ref[...], preferred_element_type=F32)
    up = jnp.dot(h, wu_ref[...], preferred_element_type=F32)
    act = (jax.nn.silu(gate) * up).astype(BF16)
    o_ref[...] += jnp.dot(act, wd_ref[...], preferred_element_type=F32)

    @pl.when(j == pl.num_programs(1) - 1)
    def _():
        y = x_ref[...] + 0.5 * o_ref[...]
        if final_norm:
            y = _rms_norm(y, gf_ref[...])
        o_ref[...] = y


def _ffn(x, g, wg, wu, wd, gf, *, final_norm, tm, tf):
    T, D = x.shape
    F = wg.shape[1]
    return pl.pallas_call(
        functools.partial(_ffn_kernel, final_norm=final_norm),
        grid=(T // tm, F // tf),
        in_specs=[
            pl.BlockSpec((tm, D), lambda i, j: (i, 0)),
            pl.BlockSpec((1, D), lambda i, j: (0, 0)),
            pl.BlockSpec((D, tf), lambda i, j: (0, j)),
            pl.BlockSpec((D, tf), lambda i, j: (0, j)),
            pl.BlockSpec((tf, D), lambda i, j: (j, 0)),
            pl.BlockSpec((1, D), lambda i, j: (0, 0)),
        ],
        out_specs=pl.BlockSpec((tm, D), lambda i, j: (i, 0)),
        out_shape=jax.ShapeDtypeStruct((T, D), F32),
        scratch_shapes=[pltpu.VMEM((tm, D), BF16)],
        compiler_params=pltpu.CompilerParams(
            dimension_semantics=("parallel", "arbitrary"),
            vmem_limit_bytes=V7X_VMEM_LIMIT_BYTES),
        name="ffn",
    )(x, g, wg, wu, wd, gf)


def _inproj_kernel(x_ref, g_ref, w_ref, uvp_ref, qkv_ref, h_ref, *, n_f32_blocks):
    j = pl.program_id(1)

    @pl.when(j == 0)
    def _():
        h_ref[...] = _rms_norm(x_ref[...], g_ref[...]).astype(BF16)

    z = jnp.dot(h_ref[...], w_ref[...], preferred_element_type=F32)

    @pl.when(j < n_f32_blocks)
    def _():
        uvp_ref[...] = z

    @pl.when(j >= n_f32_blocks)
    def _():
        qkv_ref[...] = z.astype(BF16)


def _inproj(x, g, w, *, n_f32_cols, tm, tn):
    T, D = x.shape
    N = w.shape[1]
    nf = n_f32_cols // tn
    return pl.pallas_call(
        functools.partial(_inproj_kernel, n_f32_blocks=nf),
        grid=(T // tm, N // tn),
        in_specs=[
            pl.BlockSpec((tm, D), lambda i, j: (i, 0)),
            pl.BlockSpec((1, D), lambda i, j: (0, 0)),
            pl.BlockSpec((D, tn), lambda i, j: (0, j)),
        ],
        out_specs=[
            pl.BlockSpec((tm, tn), lambda i, j: (i, jnp.minimum(j, nf - 1))),
            pl.BlockSpec((tm, tn), lambda i, j: (i, jnp.maximum(j - nf, 0))),
        ],
        out_shape=[
            jax.ShapeDtypeStruct((T, n_f32_cols), F32),
            jax.ShapeDtypeStruct((T, N - n_f32_cols), BF16),
        ],
        scratch_shapes=[pltpu.VMEM((tm, D), BF16)],
        compiler_params=pltpu.CompilerParams(
            dimension_semantics=("parallel", "arbitrary"),
            vmem_limit_bytes=V7X_VMEM_LIMIT_BYTES),
        name="inproj",
    )(x, g, w)


def _gate_pool_kernel(zu_ref, zv_ref, zp_ref, prev_ref, next_ref, sgn_ref, ws_ref, bs_ref,
                      pw_ref, ps_ref, o_ref, ext_ref, *, seq_len):
    tm, sg_width = zu_ref.shape
    pool_width = zp_ref.shape[1]
    n_heads = sg_width // HEAD_DIM
    n_groups = pool_width // HEAD_DIM

    for c in range(tm // SG_CHUNK):
        rows = slice(c * SG_CHUNK, (c + 1) * SG_CHUNK)
        for h in range(n_heads):
            cols = slice(h * HEAD_DIM, (h + 1) * HEAD_DIM)
            u = _gelu(zu_ref[rows, cols])
            v = _rms_norm(_gelu(zv_ref[rows, cols]), sgn_ref[:, cols])
            mixed = jnp.dot(ws_ref[h], v.astype(BF16), preferred_element_type=F32) + bs_ref[h]
            o_ref[rows, cols] = (u * mixed).astype(o_ref.dtype)

    tile_pos = (pl.program_id(0) * tm) % seq_len
    is_first = tile_pos == 0
    is_last = tile_pos + tm == seq_len
    ext_ref[0:POOL_HALO, :] = jnp.where(is_first, 0.0, prev_ref[...])
    ext_ref[POOL_HALO:POOL_HALO + tm, :] = zp_ref[...]
    ext_ref[POOL_HALO + tm:, :] = jnp.where(is_last, 0.0, next_ref[...])
    pos = tile_pos + lax.broadcasted_iota(jnp.int32, (tm, HEAD_DIM), 0)
    for g in range(n_groups):
        cols = slice(g * HEAD_DIM, (g + 1) * HEAD_DIM)
        half = POOL_WINDOWS[g] // 2
        total = ext_ref[POOL_HALO - half:POOL_HALO - half + tm, cols]
        for k in range(1 - half, half):
            total = total + ext_ref[POOL_HALO + k:POOL_HALO + k + tm, cols]
        cnt = jnp.minimum(pos + half, seq_len) - jnp.maximum(pos - half, 0)
        d = total / cnt.astype(F32) - zp_ref[:, cols]
        y = jnp.dot(d.astype(BF16), pw_ref[g], preferred_element_type=F32) * ps_ref[:, cols]
        o_ref[:, sg_width + g * HEAD_DIM:sg_width + (g + 1) * HEAD_DIM] = y.astype(o_ref.dtype)


def _gate_pool(uvp, sgn, ws, bs, pw, ps, *, seq_len, sg_width, pool_width, tm):
    T = uvp.shape[0]
    assert sg_width == pool_width and seq_len % tm == 0 and tm % SG_CHUNK == 0
    halo_blocks = tm // POOL_HALO
    n_heads = sg_width // HEAD_DIM
    n_groups = pool_width // HEAD_DIM
    return pl.pallas_call(
        functools.partial(_gate_pool_kernel, seq_len=seq_len),
        grid=(T // tm,),
        in_specs=[
            pl.BlockSpec((tm, sg_width), lambda i: (i, 0)),
            pl.BlockSpec((tm, sg_width), lambda i: (i, 1)),
            pl.BlockSpec((tm, pool_width), lambda i: (i, 2)),
            pl.BlockSpec((POOL_HALO, pool_width), lambda i: (jnp.maximum(i * halo_blocks - 1, 0), 2)),
            pl.BlockSpec((POOL_HALO, pool_width),
                         lambda i: (jnp.minimum((i + 1) * halo_blocks, T // POOL_HALO - 1), 2)),
            pl.BlockSpec((1, sg_width), lambda i: (0, 0)),
            pl.BlockSpec((n_heads, SG_CHUNK, SG_CHUNK), lambda i: (0, 0, 0)),
            pl.BlockSpec((n_heads, SG_CHUNK, HEAD_DIM), lambda i: (0, 0, 0)),
            pl.BlockSpec((n_groups, HEAD_DIM, HEAD_DIM), lambda i: (0, 0, 0)),
            pl.BlockSpec((1, pool_width), lambda i: (0, 0)),
        ],
        out_specs=pl.BlockSpec((tm, sg_width + pool_width), lambda i: (i, 0)),
        out_shape=jax.ShapeDtypeStruct((T, sg_width + pool_width), BF16),
        scratch_shapes=[pltpu.VMEM((tm + 2 * POOL_HALO, pool_width), F32)],
        compiler_params=pltpu.CompilerParams(
            dimension_semantics=("parallel",),
            vmem_limit_bytes=V7X_VMEM_LIMIT_BYTES),
        name="gate_pool",
    )(uvp, uvp, uvp, uvp, uvp, sgn, ws, bs, pw, ps)


def _attn_kernel(q_ref, k_ref, v_ref, b_ref, o_ref, *, n_rows):
    r0 = pl.program_id(2) * NA_ROWS_PER_BLOCK
    win_start = jnp.clip(r0 - NA_KH // 2, 0, n_rows - NA_WIN_ROWS)
    start = pl.multiple_of(win_start * GRID_W, NA_ROWS_PER_BLOCK * GRID_W)
    q = q_ref[0]
    k = k_ref[0, pl.ds(start, NA_WIN_ROWS * GRID_W), :]
    v = v_ref[0, pl.ds(start, NA_WIN_ROWS * GRID_W), :]
    s = lax.dot_general(q, k, (((1,), (1,)), ((), ())), preferred_element_type=F32)
    s = s * np.float32(HEAD_DIM ** -0.5) + b_ref[0, 0, 0]
    p = jnp.exp(s - jnp.max(s, axis=-1, keepdims=True))
    denom = jnp.sum(p, axis=-1, keepdims=True)
    o = jnp.dot(p.astype(BF16), v, preferred_element_type=F32) / denom
    o_ref[0] = o.astype(o_ref.dtype)


def _attn_bias_table(rpb, n_rows):
    R, W = NA_ROWS_PER_BLOCK, NA_WIN_ROWS
    n_dr, n_dc = 2 * NA_KH - 1, 2 * NA_KW - 1
    kh = min(NA_KH, n_rows)
    cq = np.arange(GRID_W)[:, None]
    ck = np.arange(GRID_W)[None, :]
    cs = np.clip(cq - NA_KW // 2, 0, GRID_W - NA_KW)
    col_ok = (ck >= cs) & (ck < cs + NA_KW)
    dc = np.clip(ck - cq + NA_KW - 1, 0, n_dc - 1)
    col_sel = (np.arange(n_dc)[:, None, None] == dc[None]) & col_ok[None]
    col_bias = jnp.einsum('lhdc,cqk->lhdqk', rpb, col_sel.astype(np.float32),
                          precision=lax.Precision.HIGHEST)
    col_bias = jnp.where(col_ok, col_bias, NEG)
    neg_slab = jnp.full(col_bias.shape[:2] + (1,) + col_bias.shape[3:], NEG, F32)
    col_bias = jnp.concatenate([col_bias, neg_slab], axis=2)
    row_sel = np.zeros((3, R, W, n_dr + 1), np.float32)
    for t, r0 in enumerate((0, R, n_rows - R)):
        win_start = int(np.clip(r0 - NA_KH // 2, 0, n_rows - W))
        for rq in range(R):
            r = r0 + rq
            sr = int(np.clip(r - kh // 2, 0, n_rows - kh))
            for rk in range(W):
                row = win_start + rk
                ok = sr <= row < sr + kh
                row_sel[t, rq, rk, row - r + NA_KH - 1 if ok else n_dr] = 1.0
    table = jnp.einsum('trjd,lhdqk->lthrqjk', row_sel, col_bias, precision=lax.Precision.HIGHEST)
    L, H = rpb.shape[:2]
    return table.reshape(L, 3, H, R * GRID_W, W * GRID_W)


def _attn(qkv, bias, layer, *, n_heads, seq_len):
    B = qkv.shape[0]
    n_rows = seq_len // GRID_W
    R, W = NA_ROWS_PER_BLOCK, NA_WIN_ROWS
    assert n_rows % R == 0 and n_rows >= W and n_rows // R >= 2
    n_blocks = n_rows // R

    def bias_map(b, h, i):
        kind = jnp.where(i == 0, 0, jnp.where(i == n_blocks - 1, 2, 1))
        return (layer, kind, h, 0, 0)

    return pl.pallas_call(
        functools.partial(_attn_kernel, n_rows=n_rows),
        grid=(B, n_heads, n_blocks),
        in_specs=[
            pl.BlockSpec((1, R * GRID_W, HEAD_DIM), lambda b, h, i: (b, i, h)),
            pl.BlockSpec((1, seq_len, HEAD_DIM), lambda b, h, i: (b, 0, n_heads + h)),
            pl.BlockSpec((1, seq_len, HEAD_DIM), lambda b, h, i: (b, 0, 2 * n_heads + h)),
            pl.BlockSpec((1, 1, 1, R * GRID_W, W * GRID_W), bias_map),
        ],
        out_specs=pl.BlockSpec((1, R * GRID_W, HEAD_DIM), lambda b, h, i: (b, i, h)),
        out_shape=jax.ShapeDtypeStruct((B, seq_len, n_heads * HEAD_DIM), BF16),
        compiler_params=pltpu.CompilerParams(
            dimension_semantics=("parallel", "parallel", "arbitrary"),
            vmem_limit_bytes=V7X_VMEM_LIMIT_BYTES),
        name="nbr_attn",
    )(qkv, qkv, qkv, bias)


def _outproj_kernel(x_ref, ab_ref, c_ref, w1_ref, w2_ref, o_ref):
    y = jnp.dot(ab_ref[...], w1_ref[...], preferred_element_type=F32)
    y = y + jnp.dot(c_ref[...], w2_ref[...], preferred_element_type=F32)
    o_ref[...] = x_ref[...] + y


def _outproj(x, ab, c, w, *, tm):
    T, D = x.shape
    K1, K2 = ab.shape[1], c.shape[1]
    assert K1 == K2 and w.shape[0] == K1 + K2
    return pl.pallas_call(
        _outproj_kernel,
        grid=(T // tm,),
        in_specs=[
            pl.BlockSpec((tm, D), lambda i: (i, 0)),
            pl.BlockSpec((tm, K1), lambda i: (i, 0)),
            pl.BlockSpec((tm, K2), lambda i: (i, 0)),
            pl.BlockSpec((K1, D), lambda i: (0, 0)),
            pl.BlockSpec((K2, D), lambda i: (1, 0)),
        ],
        out_specs=pl.BlockSpec((tm, D), lambda i: (i, 0)),
        out_shape=jax.ShapeDtypeStruct((T, D), F32),
        compiler_params=pltpu.CompilerParams(
            dimension_semantics=("parallel",),
            vmem_limit_bytes=V7X_VMEM_LIMIT_BYTES),
        name="outproj",
    )(x, ab, c, w, w)


def _pick_tile(n, target):
    t = min(n, target)
    while n % t:
        t //= 2
    return t


def kernel(x, ffn1_norm, ffn1_w_gate, ffn1_w_up, ffn1_w_down, mix_norm, w_in, sg_norm, sg_w, sg_b, pool_w, pool_scale, na_rpb, w_out, ffn2_norm, ffn2_w_gate, ffn2_w_up, ffn2_w_down, final_norm):
    B, S, D = x.shape
    depth = ffn1_w_gate.shape[0]
    sg_width = sg_norm.shape[1]
    pool_width = pool_scale.shape[1]
    n_attn_heads = na_rpb.shape[1]
    T = B * S
    tm_ffn = _pick_tile(T, 1024)
    tf = _pick_tile(ffn1_w_gate.shape[2], 256)
    tm_mix = _pick_tile(S, 512)

    bf = lambda w: w.astype(BF16)
    w1g, w1u, w1d = bf(ffn1_w_gate), bf(ffn1_w_up), bf(ffn1_w_down)
    w2g, w2u, w2d = bf(ffn2_w_gate), bf(ffn2_w_up), bf(ffn2_w_down)
    win, wout, sgw, pw = bf(w_in), bf(w_out), bf(sg_w), bf(pool_w)
    sgb = jnp.broadcast_to(sg_b[..., None], sg_b.shape + (HEAD_DIM,))
    gf = final_norm.reshape(1, D)
    bias = _attn_bias_table(na_rpb, S // GRID_W)

    h = x.reshape(T, D)
    for l in range(depth):
        h = _ffn(h, ffn1_norm[l].reshape(1, D), w1g[l], w1u[l], w1d[l], gf,
                 final_norm=False, tm=tm_ffn, tf=tf)
        uvp, qkv = _inproj(h, mix_norm[l].reshape(1, D), win[l],
                           n_f32_cols=2 * sg_width + pool_width, tm=tm_ffn, tn=512)
        ab = _gate_pool(uvp, sg_norm[l].reshape(1, sg_width), sgw[l], sgb[l], pw[l],
                        pool_scale[l].reshape(1, pool_width),
                        seq_len=S, sg_width=sg_width, pool_width=pool_width, tm=tm_mix)
        c = _attn(qkv.reshape(B, S, -1), bias, l, n_heads=n_attn_heads, seq_len=S)
        h = _outproj(h, ab, c.reshape(T, -1), wout[l], tm=tm_mix)
        h = _ffn(h, ffn2_norm[l].reshape(1, D), w2g[l], w2u[l], w2d[l], gf,
                 final_norm=(l == depth - 1), tm=tm_ffn, tf=tf)
    return h.reshape(B, S, D)
```

```python
import functools

import jax
import jax.numpy as jnp
import numpy as np
from jax import lax
from jax.experimental import pallas as pl
from jax.experimental.pallas import tpu as pltpu

EPS = 1e-6
NEG = -1e30
LOG2_E = 1.4426950408889634
HEAD_DIM = 128
SG_CHUNK = 128
POOL_WINDOWS = (2, 4, 8, 16)
POOL_HALO = 8
NA_KH = 8
NA_KW = 16
GRID_W = 64
NA_ROWS_PER_BLOCK = 4
NA_WIN_ROWS = 12
NA_BLOCKS_PER_ITER = 4

V7X_VMEM_LIMIT_BYTES = 56 * 1024 * 1024

F32 = jnp.float32
BF16 = jnp.bfloat16


def _rms_norm(x, g):
    return x * lax.rsqrt(jnp.mean(x * x, axis=-1, keepdims=True) + EPS) * g


def _gelu(x):
    return 0.5 * x * (1.0 + lax.erf(x * np.float32(np.sqrt(0.5))))


def _ffn_kernel(x_ref, g_ref, wg_ref, wu_ref, wd_ref, gf_ref, o_ref, h_ref, *, final_norm):
    j = pl.program_id(1)

    @pl.when(j == 0)
    def _():
        h_ref[...] = _rms_norm(x_ref[...], g_ref[...]).astype(BF16)
        o_ref[...] = jnp.zeros_like(o_ref)

    h = h_ref[...]
    gate = jnp.dot(h, wg_ref[...], preferred_element_type=F32)
    up = jnp.dot(h, wu_ref[...], preferred_element_type=F32)
    act = (jax.nn.silu(gate) * up).astype(BF16)
    o_ref[...] += jnp.dot(act, wd_ref[...], preferred_element_type=F32)

    @pl.when(j == pl.num_programs(1) - 1)
    def _():
        y = x_ref[...] + 0.5 * o_ref[...]
        if final_norm:
            y = _rms_norm(y, gf_ref[...])
        o_ref[...] = y


def _ffn(x, g, wg, wu, wd, gf, *, final_norm, tm, tf):
    T, D = x.shape
    F = wg.shape[1]
    return pl.pallas_call(
        functools.partial(_ffn_kernel, final_norm=final_norm),
        grid=(T // tm, F // tf),
        in_specs=[
            pl.BlockSpec((tm, D), lambda i, j: (i, 0)),
            pl.BlockSpec((1, D), lambda i, j: (0, 0)),
            pl.BlockSpec((D, tf), lambda i, j: (0, j)),
            pl.BlockSpec((D, tf), lambda i, j: (0, j)),
            pl.BlockSpec((tf, D), lambda i, j: (j, 0)),
            pl.BlockSpec((1, D), lambda i, j: (0, 0)),
        ],
        out_specs=pl.BlockSpec((tm, D), lambda i, j: (i, 0)),
        out_shape=jax.ShapeDtypeStruct((T, D), F32),
        scratch_shapes=[pltpu.VMEM((tm, D), BF16)],
        compiler_params=pltpu.CompilerParams(
            dimension_semantics=("parallel", "arbitrary"),
            vmem_limit_bytes=V7X_VMEM_LIMIT_BYTES),
        name="ffn",
    )(x, g, wg, wu, wd, gf)


def _inproj_kernel(x_ref, g_ref, w_ref, uvp_ref, qkv_ref, h_ref, *, n_f32_blocks):
    j = pl.program_id(1)

    @pl.when(j == 0)
    def _():
        h_ref[...] = _rms_norm(x_ref[...], g_ref[...]).astype(BF16)

    z = jnp.dot(h_ref[...], w_ref[...], preferred_element_type=F32)

    @pl.when(j < n_f32_blocks)
    def _():
        uvp_ref[...] = z

    @pl.when(j >= n_f32_blocks)
    def _():
        qkv_ref[...] = z.astype(BF16)


def _inproj(x, g, w, *, n_f32_cols, tm, tn):
    T, D = x.shape
    N = w.shape[1]
    nf = n_f32_cols // tn
    return pl.pallas_call(
        functools.partial(_inproj_kernel, n_f32_blocks=nf),
        grid=(T // tm, N // tn),
        in_specs=[
            pl.BlockSpec((tm, D), lambda i, j: (i, 0)),
            pl.BlockSpec((1, D), lambda i, j: (0, 0)),
            pl.BlockSpec((D, tn), lambda i, j: (0, j)),
        ],
        out_specs=[
            pl.BlockSpec((tm, tn), lambda i, j: (i, jnp.minimum(j, nf - 1))),
            pl.BlockSpec((tm, tn), lambda i, j: (i, jnp.maximum(j - nf, 0))),
        ],
        out_shape=[
            jax.ShapeDtypeStruct((T, n_f32_cols), F32),
            jax.ShapeDtypeStruct((T, N - n_f32_cols), BF16),
        ],
        scratch_shapes=[pltpu.VMEM((tm, D), BF16)],
        compiler_params=pltpu.CompilerParams(
            dimension_semantics=("parallel", "arbitrary"),
            vmem_limit_bytes=V7X_VMEM_LIMIT_BYTES),
        name="inproj",
    )(x, g, w)


def _gate_pool_kernel(zu_ref, zv_ref, zp_ref, prev_ref, next_ref, sgn_ref, ws_ref, bs_ref,
                      pw_ref, ps_ref, o_ref, ext_ref, *, seq_len):
    tm, sg_width = zu_ref.shape
    pool_width = zp_ref.shape[1]
    n_heads = sg_width // HEAD_DIM
    n_groups = pool_width // HEAD_DIM

    for c in range(tm // SG_CHUNK):
        rows = slice(c * SG_CHUNK, (c + 1) * SG_CHUNK)
        for h in range(n_heads):
            cols = slice(h * HEAD_DIM, (h + 1) * HEAD_DIM)
            u = _gelu(zu_ref[rows, cols])
            v = _rms_norm(_gelu(zv_ref[rows, cols]), sgn_ref[:, cols])
            mixed = jnp.dot(ws_ref[h], v.astype(BF16), preferred_element_type=F32) + bs_ref[h]
            o_ref[rows, cols] = (u * mixed).astype(o_ref.dtype)

    tile_pos = (pl.program_id(0) * tm) % seq_len
    is_first = tile_pos == 0
    is_last = tile_pos + tm == seq_len
    ext_ref[0:POOL_HALO, :] = jnp.where(is_first, 0.0, prev_ref[...])
    ext_ref[POOL_HALO:POOL_HALO + tm, :] = zp_ref[...]
    ext_ref[POOL_HALO + tm:, :] = jnp.where(is_last, 0.0, next_ref[...])
    pos = tile_pos + lax.broadcasted_iota(jnp.int32, (tm, HEAD_DIM), 0)
    for g in range(n_groups):
        cols = slice(g * HEAD_DIM, (g + 1) * HEAD_DIM)
        half = POOL_WINDOWS[g] // 2
        total = ext_ref[POOL_HALO - half:POOL_HALO - half + tm, cols]
        for k in range(1 - half, half):
            total = total + ext_ref[POOL_HALO + k:POOL_HALO + k + tm, cols]
        cnt = jnp.minimum(pos + half, seq_len) - jnp.maximum(pos - half, 0)
        d = total / cnt.astype(F32) - zp_ref[:, cols]
        y = jnp.dot(d.astype(BF16), pw_ref[g], preferred_element_type=F32) * ps_ref[:, cols]
        o_ref[:, sg_width + g * HEAD_DIM:sg_width + (g + 1) * HEAD_DIM] = y.astype(o_ref.dtype)


def _gate_pool(uvp, sgn, ws, bs, pw, ps, *, seq_len, sg_width, pool_width, tm):
    T = uvp.shape[0]
    assert sg_width == pool_width and seq_len % tm == 0 and tm % SG_CHUNK == 0
    halo_blocks = tm // POOL_HALO
    n_heads = sg_width // HEAD_DIM
    n_groups = pool_width // HEAD_DIM
    return pl.pallas_call(
        functools.partial(_gate_pool_kernel, seq_len=seq_len),
        grid=(T // tm,),
        in_specs=[
            pl.BlockSpec((tm, sg_width), lambda i: (i, 0)),
            pl.BlockSpec((tm, sg_width), lambda i: (i, 1)),
            pl.BlockSpec((tm, pool_width), lambda i: (i, 2)),
            pl.BlockSpec((POOL_HALO, pool_width), lambda i: (jnp.maximum(i * halo_blocks - 1, 0), 2)),
            pl.BlockSpec((POOL_HALO, pool_width),
                         lambda i: (jnp.minimum((i + 1) * halo_blocks, T // POOL_HALO - 1), 2)),
            pl.BlockSpec((1, sg_width), lambda i: (0, 0)),
            pl.BlockSpec((n_heads, SG_CHUNK, SG_CHUNK), lambda i: (0, 0, 0)),
            pl.BlockSpec((n_heads, SG_CHUNK, HEAD_DIM), lambda i: (0, 0, 0)),
            pl.BlockSpec((n_groups, HEAD_DIM, HEAD_DIM), lambda i: (0, 0, 0)),
            pl.BlockSpec((1, pool_width), lambda i: (0, 0)),
        ],
        out_specs=pl.BlockSpec((tm, sg_width + pool_width), lambda i: (i, 0)),
        out_shape=jax.ShapeDtypeStruct((T, sg_width + pool_width), BF16),
        scratch_shapes=[pltpu.VMEM((tm + 2 * POOL_HALO, pool_width), F32)],
        compiler_params=pltpu.CompilerParams(
            dimension_semantics=("parallel",),
            vmem_limit_bytes=V7X_VMEM_LIMIT_BYTES),
        name="gate_pool",
    )(uvp, uvp, uvp, uvp, uvp, sgn, ws, bs, pw, ps)


def _attn_kernel(q_ref, k_ref, v_ref, b_ref, o_ref, *, n_rows):
    R, W = NA_ROWS_PER_BLOCK, NA_WIN_ROWS
    n_blocks = n_rows // R

    def one_block(i):
        win_start = jnp.clip(i * R - NA_KH // 2, 0, n_rows - W)
        k_start = pl.multiple_of(win_start * GRID_W, R * GRID_W)
        q_start = pl.multiple_of(i * (R * GRID_W), R * GRID_W)
        kind = jnp.where(i == 0, 0, jnp.where(i == n_blocks - 1, 2, 1))
        q = q_ref[0, pl.ds(q_start, R * GRID_W), :]
        k = k_ref[0, pl.ds(k_start, W * GRID_W), :]
        v = v_ref[0, pl.ds(k_start, W * GRID_W), :]
        s = lax.dot_general(q, k, (((1,), (1,)), ((), ())), preferred_element_type=F32)
        s = s * np.float32(HEAD_DIM ** -0.5 * LOG2_E) + b_ref[0, kind, 0]
        p = jnp.exp2(s - jnp.max(s, axis=-1, keepdims=True))
        denom = jnp.sum(p, axis=-1, keepdims=True)
        o = jnp.dot(p.astype(BF16), v, preferred_element_type=F32) / denom
        o_ref[0, pl.ds(q_start, R * GRID_W), :] = o.astype(o_ref.dtype)

    def body(t, carry):
        for u in range(NA_BLOCKS_PER_ITER):
            one_block(t * NA_BLOCKS_PER_ITER + u)
        return carry

    lax.fori_loop(0, n_blocks // NA_BLOCKS_PER_ITER, body, 0)


def _attn_bias_table(rpb, n_rows):
    R, W = NA_ROWS_PER_BLOCK, NA_WIN_ROWS
    n_dr, n_dc = 2 * NA_KH - 1, 2 * NA_KW - 1
    kh = min(NA_KH, n_rows)
    cq = np.arange(GRID_W)[:, None]
    ck = np.arange(GRID_W)[None, :]
    cs = np.clip(cq - NA_KW // 2, 0, GRID_W - NA_KW)
    col_ok = (ck >= cs) & (ck < cs + NA_KW)
    dc = np.clip(ck - cq + NA_KW - 1, 0, n_dc - 1)
    col_sel = (np.arange(n_dc)[:, None, None] == dc[None]) & col_ok[None]
    col_bias = jnp.einsum('lhdc,cqk->lhdqk', rpb, col_sel.astype(np.float32),
                          precision=lax.Precision.HIGHEST)
    col_bias = jnp.where(col_ok, col_bias, NEG)
    neg_slab = jnp.full(col_bias.shape[:2] + (1,) + col_bias.shape[3:], NEG, F32)
    col_bias = jnp.concatenate([col_bias, neg_slab], axis=2)
    row_sel = np.zeros((3, R, W, n_dr + 1), np.float32)
    for t, r0 in enumerate((0, R, n_rows - R)):
        win_start = int(np.clip(r0 - NA_KH // 2, 0, n_rows - W))
        for rq in range(R):
            r = r0 + rq
            sr = int(np.clip(r - kh // 2, 0, n_rows - kh))
            for rk in range(W):
                row = win_start + rk
                ok = sr <= row < sr + kh
                row_sel[t, rq, rk, row - r + NA_KH - 1 if ok else n_dr] = 1.0
    table = jnp.einsum('trjd,lhdqk->lthrqjk', row_sel, col_bias, precision=lax.Precision.HIGHEST)
    L, H = rpb.shape[:2]
    return table.reshape(L, 3, H, R * GRID_W, W * GRID_W)


def _attn(qkv, bias, layer, *, n_heads, seq_len):
    B = qkv.shape[0]
    n_rows = seq_len // GRID_W
    R, W = NA_ROWS_PER_BLOCK, NA_WIN_ROWS
    assert n_rows >= W and n_rows % (R * NA_BLOCKS_PER_ITER) == 0
    return pl.pallas_call(
        functools.partial(_attn_kernel, n_rows=n_rows),
        grid=(B, n_heads),
        in_specs=[
            pl.BlockSpec((1, seq_len, HEAD_DIM), lambda b, h: (b, 0, h)),
            pl.BlockSpec((1, seq_len, HEAD_DIM), lambda b, h: (b, 0, n_heads + h)),
            pl.BlockSpec((1, seq_len, HEAD_DIM), lambda b, h: (b, 0, 2 * n_heads + h)),
            pl.BlockSpec((1, 3, 1, R * GRID_W, W * GRID_W), lambda b, h: (layer, 0, h, 0, 0)),
        ],
        out_specs=pl.BlockSpec((1, seq_len, HEAD_DIM), lambda b, h: (b, 0, h)),
        out_shape=jax.ShapeDtypeStruct((B, seq_len, n_heads * HEAD_DIM), BF16),
        compiler_params=pltpu.CompilerParams(
            dimension_semantics=("parallel", "parallel"),
            vmem_limit_bytes=V7X_VMEM_LIMIT_BYTES),
        name="nbr_attn",
    )(qkv, qkv, qkv, bias)


def _outproj_kernel(x_ref, ab_ref, c_ref, w1_ref, w2_ref, o_ref):
    y = jnp.dot(ab_ref[...], w1_ref[...], preferred_element_type=F32)
    y = y + jnp.dot(c_ref[...], w2_ref[...], preferred_element_type=F32)
    o_ref[...] = x_ref[...] + y


def _outproj(x, ab, c, w, *, tm):
    T, D = x.shape
    K1, K2 = ab.shape[1], c.shape[1]
    assert K1 == K2 and w.shape[0] == K1 + K2
    return pl.pallas_call(
        _outproj_kernel,
        grid=(T // tm,),
        in_specs=[
            pl.BlockSpec((tm, D), lambda i: (i, 0)),
            pl.BlockSpec((tm, K1), lambda i: (i, 0)),
            pl.BlockSpec((tm, K2), lambda i: (i, 0)),
            pl.BlockSpec((K1, D), lambda i: (0, 0)),
            pl.BlockSpec((K2, D), lambda i: (1, 0)),
        ],
        out_specs=pl.BlockSpec((tm, D), lambda i: (i, 0)),
        out_shape=jax.ShapeDtypeStruct((T, D), F32),
        compiler_params=pltpu.CompilerParams(
            dimension_semantics=("parallel",),
            vmem_limit_bytes=V7X_VMEM_LIMIT_BYTES),
        name="outproj",
    )(x, ab, c, w, w)


def _pick_tile(n, target):
    t = min(n, target)
    while n % t:
        t //= 2
    return t


def kernel(x, ffn1_norm, ffn1_w_gate, ffn1_w_up, ffn1_w_down, mix_norm, w_in, sg_norm, sg_w, sg_b, pool_w, pool_scale, na_rpb, w_out, ffn2_norm, ffn2_w_gate, ffn2_w_up, ffn2_w_down, final_norm):
    B, S, D = x.shape
    depth = ffn1_w_gate.shape[0]
    sg_width = sg_norm.shape[1]
    pool_width = pool_scale.shape[1]
    n_attn_heads = na_rpb.shape[1]
    T = B * S
    tm_ffn = _pick_tile(T, 1024)
    tf = _pick_tile(ffn1_w_gate.shape[2], 256)
    tm_mix = _pick_tile(S, 512)

    bf = lambda w: w.astype(BF16)
    w1g, w1u, w1d = bf(ffn1_w_gate), bf(ffn1_w_up), bf(ffn1_w_down)
    w2g, w2u, w2d = bf(ffn2_w_gate), bf(ffn2_w_up), bf(ffn2_w_down)
    win, wout, sgw, pw = bf(w_in), bf(w_out), bf(sg_w), bf(pool_w)
    sgb = jnp.broadcast_to(sg_b[..., None], sg_b.shape + (HEAD_DIM,))
    gf = final_norm.reshape(1, D)
    bias = _attn_bias_table(na_rpb * np.float32(LOG2_E), S // GRID_W)

    h = x.reshape(T, D)
    for l in range(depth):
        h = _ffn(h, ffn1_norm[l].reshape(1, D), w1g[l], w1u[l], w1d[l], gf,
                 final_norm=False, tm=tm_ffn, tf=tf)
        uvp, qkv = _inproj(h, mix_norm[l].reshape(1, D), win[l],
                           n_f32_cols=2 * sg_width + pool_width, tm=tm_ffn, tn=512)
        ab = _gate_pool(uvp, sg_norm[l].reshape(1, sg_width), sgw[l], sgb[l], pw[l],
                        pool_scale[l].reshape(1, pool_width),
                        seq_len=S, sg_width=sg_width, pool_width=pool_width, tm=tm_mix)
        c = _attn(qkv.reshape(B, S, -1), bias, l, n_heads=n_attn_heads, seq_len=S)
        h = _outproj(h, ab, c.reshape(T, -1), wout[l], tm=tm_mix)
        h = _ffn(h, ffn2_norm[l].reshape(1, D), w2g[l], w2u[l], w2d[l], gf,
                 final_norm=(l == depth - 1), tm=tm_ffn, tf=tf)
    return h.reshape(B, S, D)
```

```python
import functools

import jax
import jax.numpy as jnp
import numpy as np
from jax import lax
from jax.experimental import pallas as pl
from jax.experimental.pallas import tpu as pltpu

EPS = 1e-6
NEG = -1e30
LOG2_E = 1.4426950408889634
HEAD_DIM = 128
SG_CHUNK = 128
POOL_WINDOWS = (2, 4, 8, 16)
POOL_HALO = 8
NA_KH = 8
NA_KW = 16
GRID_W = 64
NA_ROWS_PER_BLOCK = 4
NA_WIN_ROWS = 12
NA_BLOCKS_PER_ITER = 4
EDGE_ROW_SPLIT = 2

V7X_VMEM_LIMIT_BYTES = 56 * 1024 * 1024

F32 = jnp.float32
BF16 = jnp.bfloat16


def _rms_norm(x, g):
    return x * lax.rsqrt(jnp.mean(x * x, axis=-1, keepdims=True) + EPS) * g


def _gelu(x):
    return 0.5 * x * (1.0 + lax.erf(x * np.float32(np.sqrt(0.5))))


def _ffn_kernel(x_ref, g_ref, wg_ref, wu_ref, wd_ref, gf_ref, o_ref, h_ref, *, final_norm):
    j = pl.program_id(1)
    last = pl.num_programs(1) - 1
    sub = x_ref.shape[0] // EDGE_ROW_SPLIT

    def partial_ffn(h):
        wg, wu, wd = (w[...].astype(BF16) for w in (wg_ref, wu_ref, wd_ref))
        gate = jnp.dot(h, wg, preferred_element_type=F32)
        up = jnp.dot(h, wu, preferred_element_type=F32)
        act = (jax.nn.silu(gate) * up).astype(BF16)
        return jnp.dot(act, wd, preferred_element_type=F32)

    @pl.when(j == 0)
    def _():
        for r in range(EDGE_ROW_SPLIT):
            rows = slice(r * sub, (r + 1) * sub)
            h = _rms_norm(x_ref[rows, :], g_ref[...]).astype(BF16)
            h_ref[rows, :] = h
            o_ref[rows, :] = partial_ffn(h)

    @pl.when(jnp.logical_and(j > 0, j < last))
    def _():
        o_ref[...] += partial_ffn(h_ref[...])

    @pl.when(j == last)
    def _():
        for r in range(EDGE_ROW_SPLIT):
            rows = slice(r * sub, (r + 1) * sub)
            y = x_ref[rows, :] + 0.5 * (o_ref[rows, :] + partial_ffn(h_ref[rows, :]))
            if final_norm:
                y = _rms_norm(y, gf_ref[...])
            o_ref[rows, :] = y


def _ffn(x, g, wg, wu, wd, gf, layer, *, final_norm, tm, tf):
    T, D = x.shape
    F = wg.shape[2]
    assert F // tf >= 2
    return pl.pallas_call(
        functools.partial(_ffn_kernel, final_norm=final_norm),
        grid=(T // tm, F // tf),
        in_specs=[
            pl.BlockSpec((tm, D), lambda i, j: (i, 0)),
            pl.BlockSpec((1, D), lambda i, j: (0, 0)),
            pl.BlockSpec((None, D, tf), lambda i, j: (layer, 0, j)),
            pl.BlockSpec((None, D, tf), lambda i, j: (layer, 0, j)),
            pl.BlockSpec((None, tf, D), lambda i, j: (layer, j, 0)),
            pl.BlockSpec((1, D), lambda i, j: (0, 0)),
        ],
        out_specs=pl.BlockSpec((tm, D), lambda i, j: (i, 0)),
        out_shape=jax.ShapeDtypeStruct((T, D), F32),
        scratch_shapes=[pltpu.VMEM((tm, D), BF16)],
        compiler_params=pltpu.CompilerParams(
            dimension_semantics=("parallel", "arbitrary"),
            vmem_limit_bytes=V7X_VMEM_LIMIT_BYTES),
        name="ffn",
    )(x, g, wg, wu, wd, gf)


def _inproj_kernel(x_ref, g_ref, w_ref, uvp_ref, qkv_ref, h_ref, *, n_f32_blocks):
    j = pl.program_id(1)
    sub = x_ref.shape[0] // EDGE_ROW_SPLIT

    def project(h):
        return jnp.dot(h, w_ref[...].astype(BF16), preferred_element_type=F32)

    @pl.when(j == 0)
    def _():
        for r in range(EDGE_ROW_SPLIT):
            rows = slice(r * sub, (r + 1) * sub)
            h = _rms_norm(x_ref[rows, :], g_ref[...]).astype(BF16)
            h_ref[rows, :] = h
            uvp_ref[rows, :] = project(h)

    @pl.when(jnp.logical_and(j > 0, j < n_f32_blocks))
    def _():
        uvp_ref[...] = project(h_ref[...])

    @pl.when(j >= n_f32_blocks)
    def _():
        qkv_ref[...] = project(h_ref[...]).astype(BF16)


def _inproj(x, g, w, layer, *, n_f32_cols, tm, tn):
    T, D = x.shape
    N = w.shape[2]
    nf = n_f32_cols // tn
    assert nf >= 1 and n_f32_cols % tn == 0
    return pl.pallas_call(
        functools.partial(_inproj_kernel, n_f32_blocks=nf),
        grid=(T // tm, N // tn),
        in_specs=[
            pl.BlockSpec((tm, D), lambda i, j: (i, 0)),
            pl.BlockSpec((1, D), lambda i, j: (0, 0)),
            pl.BlockSpec((None, D, tn), lambda i, j: (layer, 0, j)),
        ],
        out_specs=[
            pl.BlockSpec((tm, tn), lambda i, j: (i, jnp.minimum(j, nf - 1))),
            pl.BlockSpec((tm, tn), lambda i, j: (i, jnp.maximum(j - nf, 0))),
        ],
        out_shape=[
            jax.ShapeDtypeStruct((T, n_f32_cols), F32),
            jax.ShapeDtypeStruct((T, N - n_f32_cols), BF16),
        ],
        scratch_shapes=[pltpu.VMEM((tm, D), BF16)],
        compiler_params=pltpu.CompilerParams(
            dimension_semantics=("parallel", "arbitrary"),
            vmem_limit_bytes=V7X_VMEM_LIMIT_BYTES),
        name="inproj",
    )(x, g, w)


def _gate_pool_kernel(zu_ref, zv_ref, zp_ref, prev_ref, next_ref, sgn_ref, ws_ref, bs_ref,
                      pw_ref, ps_ref, o_ref, ext_ref, *, seq_len):
    tm, sg_width = zu_ref.shape
    pool_width = zp_ref.shape[1]
    n_heads = sg_width // HEAD_DIM
    n_groups = pool_width // HEAD_DIM

    for c in range(tm // SG_CHUNK):
        rows = slice(c * SG_CHUNK, (c + 1) * SG_CHUNK)
        for h in range(n_heads):
            cols = slice(h * HEAD_DIM, (h + 1) * HEAD_DIM)
            u = _gelu(zu_ref[rows, cols])
            v = _rms_norm(_gelu(zv_ref[rows, cols]), sgn_ref[:, cols])
            mixed = jnp.dot(ws_ref[h], v.astype(BF16), preferred_element_type=F32) + bs_ref[h]
            o_ref[rows, cols] = (u * mixed).astype(o_ref.dtype)

    tile_pos = (pl.program_id(0) * tm) % seq_len
    is_first = tile_pos == 0
    is_last = tile_pos + tm == seq_len
    ext_ref[0:POOL_HALO, :] = jnp.where(is_first, 0.0, prev_ref[...])
    ext_ref[POOL_HALO:POOL_HALO + tm, :] = zp_ref[...]
    ext_ref[POOL_HALO + tm:, :] = jnp.where(is_last, 0.0, next_ref[...])
    pos = tile_pos + lax.broadcasted_iota(jnp.int32, (tm, HEAD_DIM), 0)
    for g in range(n_groups):
        cols = slice(g * HEAD_DIM, (g + 1) * HEAD_DIM)
        half = POOL_WINDOWS[g] // 2
        total = ext_ref[POOL_HALO - half:POOL_HALO - half + tm, cols]
        for k in range(1 - half, half):
            total = total + ext_ref[POOL_HALO + k:POOL_HALO + k + tm, cols]
        cnt = jnp.minimum(pos + half, seq_len) - jnp.maximum(pos - half, 0)
        d = total / cnt.astype(F32) - zp_ref[:, cols]
        y = jnp.dot(d.astype(BF16), pw_ref[g], preferred_element_type=F32) * ps_ref[:, cols]
        o_ref[:, sg_width + g * HEAD_DIM:sg_width + (g + 1) * HEAD_DIM] = y.astype(o_ref.dtype)


def _gate_pool(uvp, sgn, ws, bs, pw, ps, *, seq_len, sg_width, pool_width, tm):
    T = uvp.shape[0]
    assert sg_width == pool_width and seq_len % tm == 0 and tm % SG_CHUNK == 0
    halo_blocks = tm // POOL_HALO
    n_heads = sg_width // HEAD_DIM
    n_groups = pool_width // HEAD_DIM
    return pl.pallas_call(
        functools.partial(_gate_pool_kernel, seq_len=seq_len),
        grid=(T // tm,),
        in_specs=[
            pl.BlockSpec((tm, sg_width), lambda i: (i, 0)),
            pl.BlockSpec((tm, sg_width), lambda i: (i, 1)),
            pl.BlockSpec((tm, pool_width), lambda i: (i, 2)),
            pl.BlockSpec((POOL_HALO, pool_width), lambda i: (jnp.maximum(i * halo_blocks - 1, 0), 2)),
            pl.BlockSpec((POOL_HALO, pool_width),
                         lambda i: (jnp.minimum((i + 1) * halo_blocks, T // POOL_HALO - 1), 2)),
            pl.BlockSpec((1, sg_width), lambda i: (0, 0)),
            pl.BlockSpec((n_heads, SG_CHUNK, SG_CHUNK), lambda i: (0, 0, 0)),
            pl.BlockSpec((n_heads, SG_CHUNK, HEAD_DIM), lambda i: (0, 0, 0)),
            pl.BlockSpec((n_groups, HEAD_DIM, HEAD_DIM), lambda i: (0, 0, 0)),
            pl.BlockSpec((1, pool_width), lambda i: (0, 0)),
        ],
        out_specs=pl.BlockSpec((tm, sg_width + pool_width), lambda i: (i, 0)),
        out_shape=jax.ShapeDtypeStruct((T, sg_width + pool_width), BF16),
        scratch_shapes=[pltpu.VMEM((tm + 2 * POOL_HALO, pool_width), F32)],
        compiler_params=pltpu.CompilerParams(
            dimension_semantics=("parallel",),
            vmem_limit_bytes=V7X_VMEM_LIMIT_BYTES),
        name="gate_pool",
    )(uvp, uvp, uvp, uvp, uvp, sgn, ws, bs, pw, ps)


def _attn_kernel(q_ref, k_ref, v_ref, b_ref, o_ref, *, n_rows):
    R, W = NA_ROWS_PER_BLOCK, NA_WIN_ROWS
    n_blocks = n_rows // R

    def one_block(i):
        win_start = jnp.clip(i * R - NA_KH // 2, 0, n_rows - W)
        k_start = pl.multiple_of(win_start * GRID_W, R * GRID_W)
        q_start = pl.multiple_of(i * (R * GRID_W), R * GRID_W)
        kind = jnp.where(i == 0, 0, jnp.where(i == n_blocks - 1, 2, 1))
        q = q_ref[0, pl.ds(q_start, R * GRID_W), :]
        k = k_ref[0, pl.ds(k_start, W * GRID_W), :]
        v = v_ref[0, pl.ds(k_start, W * GRID_W), :]
        s = lax.dot_general(q, k, (((1,), (1,)), ((), ())), preferred_element_type=F32)
        s = s * np.float32(HEAD_DIM ** -0.5 * LOG2_E) + b_ref[0, kind, 0]
        p = jnp.exp2(s - jnp.max(s, axis=-1, keepdims=True))
        denom = jnp.sum(p, axis=-1, keepdims=True)
        o = jnp.dot(p.astype(BF16), v, preferred_element_type=F32) / denom
        o_ref[0, pl.ds(q_start, R * GRID_W), :] = o.astype(o_ref.dtype)

    def body(t, carry):
        for u in range(NA_BLOCKS_PER_ITER):
            one_block(t * NA_BLOCKS_PER_ITER + u)
        return carry

    lax.fori_loop(0, n_blocks // NA_BLOCKS_PER_ITER, body, 0)


def _attn_bias_table(rpb, n_rows):
    R, W = NA_ROWS_PER_BLOCK, NA_WIN_ROWS
    n_dr, n_dc = 2 * NA_KH - 1, 2 * NA_KW - 1
    kh = min(NA_KH, n_rows)
    cq = np.arange(GRID_W)[:, None]
    ck = np.arange(GRID_W)[None, :]
    cs = np.clip(cq - NA_KW // 2, 0, GRID_W - NA_KW)
    col_ok = (ck >= cs) & (ck < cs + NA_KW)
    dc = np.clip(ck - cq + NA_KW - 1, 0, n_dc - 1)
    col_sel = (np.arange(n_dc)[:, None, None] == dc[None]) & col_ok[None]
    col_bias = jnp.einsum('lhdc,cqk->lhdqk', rpb, col_sel.astype(np.float32),
                          precision=lax.Precision.HIGHEST)
    col_bias = jnp.where(col_ok, col_bias, NEG)
    neg_slab = jnp.full(col_bias.shape[:2] + (1,) + col_bias.shape[3:], NEG, F32)
    col_bias = jnp.concatenate([col_bias, neg_slab], axis=2)
    row_sel = np.zeros((3, R, W, n_dr + 1), np.float32)
    for t, r0 in enumerate((0, R, n_rows - R)):
        win_start = int(np.clip(r0 - NA_KH // 2, 0, n_rows - W))
        for rq in range(R):
            r = r0 + rq
            sr = int(np.clip(r - kh // 2, 0, n_rows - kh))
            for rk in range(W):
                row = win_start + rk
                ok = sr <= row < sr + kh
                row_sel[t, rq, rk, row - r + NA_KH - 1 if ok else n_dr] = 1.0
    table = jnp.einsum('trjd,lhdqk->lthrqjk', row_sel, col_bias, precision=lax.Precision.HIGHEST)
    L, H = rpb.shape[:2]
    return table.reshape(L, 3, H, R * GRID_W, W * GRID_W)


def _attn(qkv, bias, layer, *, n_heads, seq_len):
    B = qkv.shape[0]
    n_rows = seq_len // GRID_W
    R, W = NA_ROWS_PER_BLOCK, NA_WIN_ROWS
    assert n_rows >= W and n_rows % (R * NA_BLOCKS_PER_ITER) == 0
    return pl.pallas_call(
        functools.partial(_attn_kernel, n_rows=n_rows),
        grid=(B, n_heads),
        in_specs=[
            pl.BlockSpec((1, seq_len, HEAD_DIM), lambda b, h: (b, 0, h)),
            pl.BlockSpec((1, seq_len, HEAD_DIM), lambda b, h: (b, 0, n_heads + h)),
            pl.BlockSpec((1, seq_len, HEAD_DIM), lambda b, h: (b, 0, 2 * n_heads + h)),
            pl.BlockSpec((1, 3, 1, R * GRID_W, W * GRID_W), lambda b, h: (layer, 0, h, 0, 0)),
        ],
        out_specs=pl.BlockSpec((1, seq_len, HEAD_DIM), lambda b, h: (b, 0, h)),
        out_shape=jax.ShapeDtypeStruct((B, seq_len, n_heads * HEAD_DIM), BF16),
        compiler_params=pltpu.CompilerParams(
            dimension_semantics=("parallel", "parallel"),
            vmem_limit_bytes=V7X_VMEM_LIMIT_BYTES),
        name="nbr_attn",
    )(qkv, qkv, qkv, bias)


def _outproj_kernel(x_ref, ab_ref, c_ref, w1_ref, w2_ref, o_ref):
    y = jnp.dot(ab_ref[...], w1_ref[...], preferred_element_type=F32)
    y = y + jnp.dot(c_ref[...], w2_ref[...], preferred_element_type=F32)
    o_ref[...] = x_ref[...] + y


def _outproj(x, ab, c, w, *, tm):
    T, D = x.shape
    K1, K2 = ab.shape[1], c.shape[1]
    assert K1 == K2 and w.shape[0] == K1 + K2
    return pl.pallas_call(
        _outproj_kernel,
        grid=(T // tm,),
        in_specs=[
            pl.BlockSpec((tm, D), lambda i: (i, 0)),
            pl.BlockSpec((tm, K1), lambda i: (i, 0)),
            pl.BlockSpec((tm, K2), lambda i: (i, 0)),
            pl.BlockSpec((K1, D), lambda i: (0, 0)),
            pl.BlockSpec((K2, D), lambda i: (1, 0)),
        ],
        out_specs=pl.BlockSpec((tm, D), lambda i: (i, 0)),
        out_shape=jax.ShapeDtypeStruct((T, D), F32),
        compiler_params=pltpu.CompilerParams(
            dimension_semantics=("parallel",),
            vmem_limit_bytes=V7X_VMEM_LIMIT_BYTES),
        name="outproj",
    )(x, ab, c, w, w)


def _pick_tile(n, target):
    t = min(n, target)
    while n % t:
        t //= 2
    return t


def kernel(x, ffn1_norm, ffn1_w_gate, ffn1_w_up, ffn1_w_down, mix_norm, w_in, sg_norm, sg_w, sg_b, pool_w, pool_scale, na_rpb, w_out, ffn2_norm, ffn2_w_gate, ffn2_w_up, ffn2_w_down, final_norm):
    B, S, D = x.shape
    depth = ffn1_w_gate.shape[0]
    sg_width = sg_norm.shape[1]
    pool_width = pool_scale.shape[1]
    n_attn_heads = na_rpb.shape[1]
    T = B * S
    tm_ffn = _pick_tile(T, 1024)
    tf = _pick_tile(ffn1_w_gate.shape[2], 256)
    tm_mix = _pick_tile(S, 512)

    bf = lambda w: w.astype(BF16)
    wout, sgw, pw = bf(w_out), bf(sg_w), bf(pool_w)
    sgb = jnp.broadcast_to(sg_b[..., None], sg_b.shape + (HEAD_DIM,))
    gf = final_norm.reshape(1, D)
    bias = _attn_bias_table(na_rpb * np.float32(LOG2_E), S // GRID_W)

    h = x.reshape(T, D)
    for l in range(depth):
        h = _ffn(h, ffn1_norm[l].reshape(1, D), ffn1_w_gate, ffn1_w_up, ffn1_w_down, gf, l,
                 final_norm=False, tm=tm_ffn, tf=tf)
        uvp, qkv = _inproj(h, mix_norm[l].reshape(1, D), w_in, l,
                           n_f32_cols=2 * sg_width + pool_width, tm=tm_ffn, tn=512)
        ab = _gate_pool(uvp, sg_norm[l].reshape(1, sg_width), sgw[l], sgb[l], pw[l],
                        pool_scale[l].reshape(1, pool_width),
                        seq_len=S, sg_width=sg_width, pool_width=pool_width, tm=tm_mix)
        c = _attn(qkv.reshape(B, S, -1), bias, l, n_heads=n_attn_heads, seq_len=S)
        h = _outproj(h, ab, c.reshape(T, -1), wout[l], tm=tm_mix)
        h = _ffn(h, ffn2_norm[l].reshape(1, D), ffn2_w_gate, ffn2_w_up, ffn2_w_down, gf, l,
                 final_norm=(l == depth - 1), tm=tm_ffn, tf=tf)
    return h.reshape(B, S, D)
```

```python
import functools

import jax
import jax.numpy as jnp
import numpy as np
from jax import lax
from jax.experimental import pallas as pl
from jax.experimental.pallas import tpu as pltpu

EPS = 1e-6
NEG = -1e30
LOG2_E = 1.4426950408889634
HEAD_DIM = 128
SG_CHUNK = 128
POOL_WINDOWS = (2, 4, 8, 16)
POOL_HALO = 8
NA_KH = 8
NA_KW = 16
GRID_W = 64
NA_ROWS_PER_BLOCK = 4
NA_WIN_ROWS = 12
NA_BLOCKS_PER_ITER = 8
EDGE_ROW_SPLIT = 2

V7X_VMEM_LIMIT_BYTES = 56 * 1024 * 1024

F32 = jnp.float32
BF16 = jnp.bfloat16


def _rms_norm(x, g):
    return x * lax.rsqrt(jnp.mean(x * x, axis=-1, keepdims=True) + EPS) * g


def _gelu(x):
    return 0.5 * x * (1.0 + lax.erf(x * np.float32(np.sqrt(0.5))))


def _ffn_kernel(x_ref, g_ref, wg_ref, wu_ref, wd_ref, gf_ref, o_ref, h_ref, *, final_norm):
    j = pl.program_id(1)
    last = pl.num_programs(1) - 1
    sub = x_ref.shape[0] // EDGE_ROW_SPLIT

    def partial_ffn(h):
        wg, wu, wd = (w[...].astype(BF16) for w in (wg_ref, wu_ref, wd_ref))
        gate = jnp.dot(h, wg, preferred_element_type=F32)
        up = jnp.dot(h, wu, preferred_element_type=F32)
        act = (jax.nn.silu(gate) * up).astype(BF16)
        return jnp.dot(act, wd, preferred_element_type=F32)

    @pl.when(j == 0)
    def _():
        for r in range(EDGE_ROW_SPLIT):
            rows = slice(r * sub, (r + 1) * sub)
            h = _rms_norm(x_ref[rows, :], g_ref[...]).astype(BF16)
            h_ref[rows, :] = h
            o_ref[rows, :] = partial_ffn(h)

    @pl.when(jnp.logical_and(j > 0, j < last))
    def _():
        o_ref[...] += partial_ffn(h_ref[...])

    @pl.when(j == last)
    def _():
        for r in range(EDGE_ROW_SPLIT):
            rows = slice(r * sub, (r + 1) * sub)
            y = x_ref[rows, :] + 0.5 * (o_ref[rows, :] + partial_ffn(h_ref[rows, :]))
            if final_norm:
                y = _rms_norm(y, gf_ref[...])
            o_ref[rows, :] = y


def _ffn(x, g, wg, wu, wd, gf, layer, *, final_norm, tm, tf):
    T, D = x.shape
    F = wg.shape[2]
    assert F // tf >= 2
    return pl.pallas_call(
        functools.partial(_ffn_kernel, final_norm=final_norm),
        grid=(T // tm, F // tf),
        in_specs=[
            pl.BlockSpec((tm, D), lambda i, j: (i, 0)),
            pl.BlockSpec((1, D), lambda i, j: (0, 0)),
            pl.BlockSpec((None, D, tf), lambda i, j: (layer, 0, j)),
            pl.BlockSpec((None, D, tf), lambda i, j: (layer, 0, j)),
            pl.BlockSpec((None, tf, D), lambda i, j: (layer, j, 0)),
            pl.BlockSpec((1, D), lambda i, j: (0, 0)),
        ],
        out_specs=pl.BlockSpec((tm, D), lambda i, j: (i, 0)),
        out_shape=jax.ShapeDtypeStruct((T, D), F32),
        scratch_shapes=[pltpu.VMEM((tm, D), BF16)],
        compiler_params=pltpu.CompilerParams(
            dimension_semantics=("parallel", "arbitrary"),
            vmem_limit_bytes=V7X_VMEM_LIMIT_BYTES),
        name="ffn",
    )(x, g, wg, wu, wd, gf)


def _inproj_kernel(x_ref, g_ref, w_ref, uvp_ref, qkv_ref, h_ref, *, n_f32_blocks):
    j = pl.program_id(1)
    sub = x_ref.shape[0] // EDGE_ROW_SPLIT

    def project(h):
        return jnp.dot(h, w_ref[...].astype(BF16), preferred_element_type=F32)

    @pl.when(j == 0)
    def _():
        for r in range(EDGE_ROW_SPLIT):
            rows = slice(r * sub, (r + 1) * sub)
            h = _rms_norm(x_ref[rows, :], g_ref[...]).astype(BF16)
            h_ref[rows, :] = h
            uvp_ref[rows, :] = project(h)

    @pl.when(jnp.logical_and(j > 0, j < n_f32_blocks))
    def _():
        uvp_ref[...] = project(h_ref[...])

    @pl.when(j >= n_f32_blocks)
    def _():
        qkv_ref[...] = project(h_ref[...]).astype(BF16)


def _inproj(x, g, w, layer, *, n_f32_cols, tm, tn):
    T, D = x.shape
    N = w.shape[2]
    nf = n_f32_cols // tn
    assert nf >= 1 and n_f32_cols % tn == 0
    return pl.pallas_call(
        functools.partial(_inproj_kernel, n_f32_blocks=nf),
        grid=(T // tm, N // tn),
        in_specs=[
            pl.BlockSpec((tm, D), lambda i, j: (i, 0)),
            pl.BlockSpec((1, D), lambda i, j: (0, 0)),
            pl.BlockSpec((None, D, tn), lambda i, j: (layer, 0, j)),
        ],
        out_specs=[
            pl.BlockSpec((tm, tn), lambda i, j: (i, jnp.minimum(j, nf - 1))),
            pl.BlockSpec((tm, tn), lambda i, j: (i, jnp.maximum(j - nf, 0))),
        ],
        out_shape=[
            jax.ShapeDtypeStruct((T, n_f32_cols), F32),
            jax.ShapeDtypeStruct((T, N - n_f32_cols), BF16),
        ],
        scratch_shapes=[pltpu.VMEM((tm, D), BF16)],
        compiler_params=pltpu.CompilerParams(
            dimension_semantics=("parallel", "arbitrary"),
            vmem_limit_bytes=V7X_VMEM_LIMIT_BYTES),
        name="inproj",
    )(x, g, w)


def _gate_pool_tile(zu_ref, zv_ref, zp_ref, prev_ref, next_ref, sgn_ref, ws_ref, bs_ref,
                    pw_ref, ps_ref, o_ref, ext_ref, *, seq_len):
    tm, sg_width = zu_ref.shape
    pool_width = zp_ref.shape[1]
    n_heads = sg_width // HEAD_DIM
    n_groups = pool_width // HEAD_DIM

    for c in range(tm // SG_CHUNK):
        rows = slice(c * SG_CHUNK, (c + 1) * SG_CHUNK)
        for h in range(n_heads):
            cols = slice(h * HEAD_DIM, (h + 1) * HEAD_DIM)
            u = _gelu(zu_ref[rows, cols])
            v = _rms_norm(_gelu(zv_ref[rows, cols]), sgn_ref[:, cols])
            mixed = jnp.dot(ws_ref[h], v.astype(BF16), preferred_element_type=F32) + bs_ref[h]
            o_ref[rows, cols] = (u * mixed).astype(o_ref.dtype)

    tile_pos = (pl.program_id(0) * tm) % seq_len
    is_first = tile_pos == 0
    is_last = tile_pos + tm == seq_len
    ext_ref[0:POOL_HALO, :] = jnp.where(is_first, 0.0, prev_ref[...])
    ext_ref[POOL_HALO:POOL_HALO + tm, :] = zp_ref[...]
    ext_ref[POOL_HALO + tm:, :] = jnp.where(is_last, 0.0, next_ref[...])
    pos = tile_pos + lax.broadcasted_iota(jnp.int32, (tm, HEAD_DIM), 0)
    for g in range(n_groups):
        cols = slice(g * HEAD_DIM, (g + 1) * HEAD_DIM)
        half = POOL_WINDOWS[g] // 2
        total = ext_ref[POOL_HALO - half:POOL_HALO - half + tm, cols]
        for k in range(1 - half, half):
            total = total + ext_ref[POOL_HALO + k:POOL_HALO + k + tm, cols]
        cnt = jnp.minimum(pos + half, seq_len) - jnp.maximum(pos - half, 0)
        d = total / cnt.astype(F32) - zp_ref[:, cols]
        y = jnp.dot(d.astype(BF16), pw_ref[g], preferred_element_type=F32) * ps_ref[:, cols]
        o_ref[:, sg_width + g * HEAD_DIM:sg_width + (g + 1) * HEAD_DIM] = y.astype(o_ref.dtype)


def _attn_kernel(q_ref, k_ref, v_ref, b_ref, o_ref, *, n_rows):
    R, W = NA_ROWS_PER_BLOCK, NA_WIN_ROWS
    n_blocks = n_rows // R

    def one_block(i):
        win_start = jnp.clip(i * R - NA_KH // 2, 0, n_rows - W)
        k_start = pl.multiple_of(win_start * GRID_W, R * GRID_W)
        q_start = pl.multiple_of(i * (R * GRID_W), R * GRID_W)
        kind = jnp.where(i == 0, 0, jnp.where(i == n_blocks - 1, 2, 1))
        q = q_ref[0, pl.ds(q_start, R * GRID_W), :]
        k = k_ref[0, pl.ds(k_start, W * GRID_W), :]
        v = v_ref[0, pl.ds(k_start, W * GRID_W), :]
        s = lax.dot_general(q, k, (((1,), (1,)), ((), ())), preferred_element_type=F32)
        s = s * np.float32(HEAD_DIM ** -0.5 * LOG2_E) + b_ref[0, kind, 0]
        p = jnp.exp2(s - jnp.max(s, axis=-1, keepdims=True))
        denom = jnp.sum(p, axis=-1, keepdims=True)
        o = jnp.dot(p.astype(BF16), v, preferred_element_type=F32) / denom
        o_ref[0, pl.ds(q_start, R * GRID_W), :] = o.astype(o_ref.dtype)

    def body(t, carry):
        for u in range(NA_BLOCKS_PER_ITER):
            one_block(t * NA_BLOCKS_PER_ITER + u)
        return carry

    lax.fori_loop(0, n_blocks // NA_BLOCKS_PER_ITER, body, 0)


def _bias_rows_kernel(cb_ref, o_ref, *, plan):
    low_half = lax.broadcasted_iota(jnp.int32, (GRID_W, 2 * GRID_W), 1) < GRID_W
    for kind, kind_plan in enumerate(plan):
        for rq, slabs in enumerate(kind_plan):
            for jp in range(len(slabs) // 2):
                pair = jnp.where(low_half, cb_ref[0, 0, slabs[2 * jp]], cb_ref[0, 0, slabs[2 * jp + 1]])
                o_ref[0, kind, 0, rq * GRID_W:(rq + 1) * GRID_W, jp * 2 * GRID_W:(jp + 1) * 2 * GRID_W] = pair


def _attn_bias_table(rpb, n_rows):
    R, W = NA_ROWS_PER_BLOCK, NA_WIN_ROWS
    n_dr, n_dc = 2 * NA_KH - 1, 2 * NA_KW - 1
    kh = min(NA_KH, n_rows)
    cq = np.arange(GRID_W)[:, None]
    ck = np.arange(GRID_W)[None, :]
    cs = np.clip(cq - NA_KW // 2, 0, GRID_W - NA_KW)
    col_ok = (ck >= cs) & (ck < cs + NA_KW)
    dc = np.clip(ck - cq + NA_KW - 1, 0, n_dc - 1)
    col_sel = (np.arange(n_dc)[:, None, None] == dc[None]) & col_ok[None]
    col_bias = jnp.einsum('lhdc,cqk->lhdqk', rpb, col_sel.astype(np.float32),
                          precision=lax.Precision.HIGHEST)
    col_bias = jnp.where(col_ok, col_bias, NEG)
    neg_slab = jnp.full(col_bias.shape[:2] + (1,) + col_bias.shape[3:], NEG, F32)
    col_bias = jnp.concatenate([col_bias, neg_slab], axis=2)
    col_bias = jnp.concatenate([col_bias, col_bias], axis=-1)
    plan = []
    for r0 in (0, R, n_rows - R):
        win_start = int(np.clip(r0 - NA_KH // 2, 0, n_rows - W))
        kind_plan = []
        for rq in range(R):
            r = r0 + rq
            sr = int(np.clip(r - kh // 2, 0, n_rows - kh))
            rows = [win_start + rk for rk in range(W)]
            kind_plan.append(tuple(row - r + NA_KH - 1 if sr <= row < sr + kh else n_dr for row in rows))
        plan.append(tuple(kind_plan))
    L, H = rpb.shape[:2]
    return pl.pallas_call(
        functools.partial(_bias_rows_kernel, plan=tuple(plan)),
        grid=(L, H),
        in_specs=[pl.BlockSpec((1, 1, n_dr + 1, GRID_W, 2 * GRID_W), lambda l, h: (l, h, 0, 0, 0))],
        out_specs=pl.BlockSpec((1, 3, 1, R * GRID_W, W * GRID_W), lambda l, h: (l, 0, h, 0, 0)),
        out_shape=jax.ShapeDtypeStruct((L, 3, H, R * GRID_W, W * GRID_W), F32),
        compiler_params=pltpu.CompilerParams(dimension_semantics=("parallel", "parallel")),
        name="attn_bias_rows",
    )(col_bias)


def _attn(qkv, bias, layer, *, n_heads, seq_len):
    B = qkv.shape[0]
    n_rows = seq_len // GRID_W
    R, W = NA_ROWS_PER_BLOCK, NA_WIN_ROWS
    assert n_rows >= W and n_rows % (R * NA_BLOCKS_PER_ITER) == 0
    return pl.pallas_call(
        functools.partial(_attn_kernel, n_rows=n_rows),
        grid=(B, n_heads),
        in_specs=[
            pl.BlockSpec((1, seq_len, HEAD_DIM), lambda b, h: (b, 0, h)),
            pl.BlockSpec((1, seq_len, HEAD_DIM), lambda b, h: (b, 0, n_heads + h)),
            pl.BlockSpec((1, seq_len, HEAD_DIM), lambda b, h: (b, 0, 2 * n_heads + h)),
            pl.BlockSpec((1, 3, 1, R * GRID_W, W * GRID_W), lambda b, h: (layer, 0, h, 0, 0)),
        ],
        out_specs=pl.BlockSpec((1, seq_len, HEAD_DIM), lambda b, h: (b, 0, h)),
        out_shape=jax.ShapeDtypeStruct((B, seq_len, n_heads * HEAD_DIM), BF16),
        compiler_params=pltpu.CompilerParams(
            dimension_semantics=("parallel", "parallel"),
            vmem_limit_bytes=V7X_VMEM_LIMIT_BYTES),
        name="nbr_attn",
    )(qkv, qkv, qkv, bias)


def _mix_out_kernel(x_ref, zu_ref, zv_ref, zp_ref, prev_ref, next_ref, c_ref, sgn_ref, ws_ref, bs_ref,
                    pw_ref, ps_ref, w1_ref, w2_ref, o_ref, ext_ref, ab_ref, *, seq_len):
    y_c = jnp.dot(c_ref[...], w2_ref[...], preferred_element_type=F32)
    _gate_pool_tile(zu_ref, zv_ref, zp_ref, prev_ref, next_ref, sgn_ref, ws_ref, bs_ref,
                    pw_ref, ps_ref, ab_ref, ext_ref, seq_len=seq_len)
    y_ab = jnp.dot(ab_ref[...], w1_ref[...], preferred_element_type=F32)
    o_ref[...] = x_ref[...] + (y_ab + y_c)


def _mix_out(x, uvp, c, sgn, ws, bs, pw, ps, w, *, seq_len, sg_width, pool_width, tm):
    T, D = x.shape
    K1, K2 = sg_width + pool_width, c.shape[1]
    assert sg_width == pool_width and seq_len % tm == 0 and tm % SG_CHUNK == 0
    assert K1 == K2 and w.shape[0] == K1 + K2
    halo_blocks = tm // POOL_HALO
    n_heads = sg_width // HEAD_DIM
    n_groups = pool_width // HEAD_DIM
    return pl.pallas_call(
        functools.partial(_mix_out_kernel, seq_len=seq_len),
        grid=(T // tm,),
        in_specs=[
            pl.BlockSpec((tm, D), lambda i: (i, 0)),
            pl.BlockSpec((tm, sg_width), lambda i: (i, 0)),
            pl.BlockSpec((tm, sg_width), lambda i: (i, 1)),
            pl.BlockSpec((tm, pool_width), lambda i: (i, 2)),
            pl.BlockSpec((POOL_HALO, pool_width), lambda i: (jnp.maximum(i * halo_blocks - 1, 0), 2)),
            pl.BlockSpec((POOL_HALO, pool_width),
                         lambda i: (jnp.minimum((i + 1) * halo_blocks, T // POOL_HALO - 1), 2)),
            pl.BlockSpec((tm, K2), lambda i: (i, 0)),
            pl.BlockSpec((1, sg_width), lambda i: (0, 0)),
            pl.BlockSpec((n_heads, SG_CHUNK, SG_CHUNK), lambda i: (0, 0, 0)),
            pl.BlockSpec((n_heads, SG_CHUNK, HEAD_DIM), lambda i: (0, 0, 0)),
            pl.BlockSpec((n_groups, HEAD_DIM, HEAD_DIM), lambda i: (0, 0, 0)),
            pl.BlockSpec((1, pool_width), lambda i: (0, 0)),
            pl.BlockSpec((K1, D), lambda i: (0, 0)),
            pl.BlockSpec((K2, D), lambda i: (1, 0)),
        ],
        out_specs=pl.BlockSpec((tm, D), lambda i: (i, 0)),
        out_shape=jax.ShapeDtypeStruct((T, D), F32),
        scratch_shapes=[pltpu.VMEM((tm + 2 * POOL_HALO, pool_width), F32),
                        pltpu.VMEM((tm, K1), BF16)],
        compiler_params=pltpu.CompilerParams(
            dimension_semantics=("parallel",),
            vmem_limit_bytes=V7X_VMEM_LIMIT_BYTES),
        name="mix_out",
    )(x, uvp, uvp, uvp, uvp, uvp, c, sgn, ws, bs, pw, ps, w, w)


def _pick_tile(n, target):
    t = min(n, target)
    while n % t:
        t //= 2
    return t


def kernel(x, ffn1_norm, ffn1_w_gate, ffn1_w_up, ffn1_w_down, mix_norm, w_in, sg_norm, sg_w, sg_b, pool_w, pool_scale, na_rpb, w_out, ffn2_norm, ffn2_w_gate, ffn2_w_up, ffn2_w_down, final_norm):
    B, S, D = x.shape
    depth = ffn1_w_gate.shape[0]
    sg_width = sg_norm.shape[1]
    pool_width = pool_scale.shape[1]
    n_attn_heads = na_rpb.shape[1]
    T = B * S
    tm_ffn = _pick_tile(T, 1024)
    tf = _pick_tile(ffn1_w_gate.shape[2], 256)
    tm_mix = _pick_tile(S, 512)

    bf = lambda w: w.astype(BF16)
    wout, sgw, pw = bf(w_out), bf(sg_w), bf(pool_w)
    sgb = jnp.broadcast_to(sg_b[..., None], sg_b.shape + (HEAD_DIM,))
    gf = final_norm.reshape(1, D)
    bias = _attn_bias_table(na_rpb * np.float32(LOG2_E), S // GRID_W)

    h = x.reshape(T, D)
    for l in range(depth):
        h = _ffn(h, ffn1_norm[l].reshape(1, D), ffn1_w_gate, ffn1_w_up, ffn1_w_down, gf, l,
                 final_norm=False, tm=tm_ffn, tf=tf)
        uvp, qkv = _inproj(h, mix_norm[l].reshape(1, D), w_in, l,
                           n_f32_cols=2 * sg_width + pool_width, tm=tm_ffn, tn=512)
        c = _attn(qkv.reshape(B, S, -1), bias, l, n_heads=n_attn_heads, seq_len=S)
        h = _mix_out(h, uvp, c.reshape(T, -1), sg_norm[l].reshape(1, sg_width), sgw[l], sgb[l], pw[l],
                     pool_scale[l].reshape(1, pool_width), wout[l],
                     seq_len=S, sg_width=sg_width, pool_width=pool_width, tm=tm_mix)
        h = _ffn(h, ffn2_norm[l].reshape(1, D), ffn2_w_gate, ffn2_w_up, ffn2_w_down, gf, l,
                 final_norm=(l == depth - 1), tm=tm_ffn, tf=tf)
    return h.reshape(B, S, D)
```

```python
import functools

import jax
import jax.numpy as jnp
import numpy as np
from jax import lax
from jax.experimental import pallas as pl
from jax.experimental.pallas import tpu as pltpu

EPS = 1e-6
NEG = -1e30
LOG2_E = 1.4426950408889634
HEAD_DIM = 128
SG_CHUNK = 128
POOL_WINDOWS = (2, 4, 8, 16)
POOL_HALO = 8
NA_KH = 8
NA_KW = 16
GRID_W = 64
NA_ROWS_PER_BLOCK = 4
NA_WIN_ROWS = 12
NA_BLOCKS_PER_ITER = 8
EDGE_ROW_SPLIT = 2

V7X_VMEM_LIMIT_BYTES = 58 * 1024 * 1024

F32 = jnp.float32
BF16 = jnp.bfloat16


def _rms_norm(x, g):
    return x * lax.rsqrt(jnp.mean(x * x, axis=-1, keepdims=True) + EPS) * g


def _gelu(x):
    return 0.5 * x * (1.0 + lax.erf(x * np.float32(np.sqrt(0.5))))


def _ffn_kernel(x_ref, g_ref, wg_ref, wu_ref, wd_ref, gf_ref, *rest, final_norm, emit_bf16_weights):
    if emit_bf16_weights:
        o_ref, wg16_ref, wu16_ref, wd16_ref, h_ref = rest
        for src, dst in ((wg_ref, wg16_ref), (wu_ref, wu16_ref), (wd_ref, wd16_ref)):
            dst[...] = src[...].astype(BF16)
        wg_ref, wu_ref, wd_ref = wg16_ref, wu16_ref, wd16_ref
    else:
        _, o_ref, h_ref = rest
    j = pl.program_id(1)
    last = pl.num_programs(1) - 1
    sub = x_ref.shape[0] // EDGE_ROW_SPLIT

    def partial_ffn(h):
        wg, wu, wd = wg_ref[...], wu_ref[...], wd_ref[...]
        gate = jnp.dot(h, wg, preferred_element_type=F32)
        up = jnp.dot(h, wu, preferred_element_type=F32)
        act = (jax.nn.silu(gate) * up).astype(BF16)
        return jnp.dot(act, wd, preferred_element_type=F32)

    @pl.when(j == 0)
    def _():
        for r in range(EDGE_ROW_SPLIT):
            rows = slice(r * sub, (r + 1) * sub)
            h = _rms_norm(x_ref[rows, :], g_ref[...]).astype(BF16)
            h_ref[rows, :] = h
            o_ref[rows, :] = partial_ffn(h)

    @pl.when(jnp.logical_and(j > 0, j < last))
    def _():
        o_ref[...] += partial_ffn(h_ref[...])

    @pl.when(j == last)
    def _():
        for r in range(EDGE_ROW_SPLIT):
            rows = slice(r * sub, (r + 1) * sub)
            y = x_ref[rows, :] + 0.5 * (o_ref[rows, :] + partial_ffn(h_ref[rows, :]))
            if final_norm:
                y = _rms_norm(y, gf_ref[...])
            o_ref[rows, :] = y


def _ffn(x, g, wg, wu, wd, gf, layer, *, final_norm, tm, tf):
    T, D = x.shape
    F = wg.shape[2]
    assert F // tf >= 2
    n_tiles = T // tm
    params = pltpu.CompilerParams(dimension_semantics=("parallel", "arbitrary"),
                                  vmem_limit_bytes=V7X_VMEM_LIMIT_BYTES)
    vec_spec = pl.BlockSpec((1, D), lambda i, j: (0, 0))
    y, wg16, wu16, wd16 = pl.pallas_call(
        functools.partial(_ffn_kernel, final_norm=final_norm, emit_bf16_weights=True),
        grid=(1, F // tf),
        in_specs=[
            pl.BlockSpec((tm, D), lambda i, j: (0, 0)),
            vec_spec,
            pl.BlockSpec((None, D, tf), lambda i, j: (layer, 0, j)),
            pl.BlockSpec((None, D, tf), lambda i, j: (layer, 0, j)),
            pl.BlockSpec((None, tf, D), lambda i, j: (layer, j, 0)),
            vec_spec,
        ],
        out_specs=[
            pl.BlockSpec((tm, D), lambda i, j: (0, 0)),
            pl.BlockSpec((D, tf), lambda i, j: (0, j)),
            pl.BlockSpec((D, tf), lambda i, j: (0, j)),
            pl.BlockSpec((tf, D), lambda i, j: (j, 0)),
        ],
        out_shape=[
            jax.ShapeDtypeStruct((T, D), F32),
            jax.ShapeDtypeStruct((D, F), BF16),
            jax.ShapeDtypeStruct((D, F), BF16),
            jax.ShapeDtypeStruct((F, D), BF16),
        ],
        scratch_shapes=[pltpu.VMEM((tm, D), BF16)],
        compiler_params=params,
        name="ffn_first_tile",
    )(x, g, wg, wu, wd, gf)
    if n_tiles == 1:
        return y
    return pl.pallas_call(
        functools.partial(_ffn_kernel, final_norm=final_norm, emit_bf16_weights=False),
        grid=(n_tiles - 1, F // tf),
        in_specs=[
            pl.BlockSpec((tm, D), lambda i, j: (i + 1, 0)),
            vec_spec,
            pl.BlockSpec((D, tf), lambda i, j: (0, j)),
            pl.BlockSpec((D, tf), lambda i, j: (0, j)),
            pl.BlockSpec((tf, D), lambda i, j: (j, 0)),
            vec_spec,
            pl.BlockSpec(memory_space=pl.ANY),
        ],
        out_specs=pl.BlockSpec((tm, D), lambda i, j: (i + 1, 0)),
        out_shape=jax.ShapeDtypeStruct((T, D), F32),
        input_output_aliases={6: 0},
        scratch_shapes=[pltpu.VMEM((tm, D), BF16)],
        compiler_params=params,
        name="ffn",
    )(x, g, wg16, wu16, wd16, gf, y)


def _inproj_kernel(x_ref, g_ref, w_ref, uvp_ref, qkv_ref, h_ref, *, tn):
    sub = x_ref.shape[0] // EDGE_ROW_SPLIT
    n_f32 = uvp_ref.shape[1]

    def project(h, col):
        return jnp.dot(h, w_ref[:, col:col + tn], preferred_element_type=F32)

    def store(rows, col, z):
        if col < n_f32:
            uvp_ref[rows, col:col + tn] = z
        else:
            qkv_ref[rows, col - n_f32:col - n_f32 + tn] = z.astype(BF16)

    for r in range(EDGE_ROW_SPLIT):
        rows = slice(r * sub, (r + 1) * sub)
        h = _rms_norm(x_ref[rows, :], g_ref[...]).astype(BF16)
        h_ref[rows, :] = h
        store(rows, 0, project(h, 0))
    for col in range(tn, w_ref.shape[1], tn):
        store(slice(None), col, project(h_ref[...], col))


def _inproj(x, g, w, layer, *, n_f32_cols, tm, tn):
    T, D = x.shape
    N = w.shape[2]
    assert n_f32_cols % tn == 0 and N % tn == 0
    return pl.pallas_call(
        functools.partial(_inproj_kernel, tn=tn),
        grid=(T // tm,),
        in_specs=[
            pl.BlockSpec((tm, D), lambda i: (i, 0)),
            pl.BlockSpec((1, D), lambda i: (0, 0)),
            pl.BlockSpec((None, D, N), lambda i: (layer, 0, 0), pipeline_mode=pl.Buffered(1)),
        ],
        out_specs=[
            pl.BlockSpec((tm, n_f32_cols), lambda i: (i, 0)),
            pl.BlockSpec((tm, N - n_f32_cols), lambda i: (i, 0)),
        ],
        out_shape=[
            jax.ShapeDtypeStruct((T, n_f32_cols), F32),
            jax.ShapeDtypeStruct((T, N - n_f32_cols), BF16),
        ],
        scratch_shapes=[pltpu.VMEM((tm, D), BF16)],
        compiler_params=pltpu.CompilerParams(
            dimension_semantics=("parallel",),
            vmem_limit_bytes=V7X_VMEM_LIMIT_BYTES),
        name="inproj",
    )(x, g, w)


def _gate_pool_tile(zu_ref, zv_ref, zp_ref, prev_ref, next_ref, sgn_ref, ws_ref, bs_ref,
                    pw_ref, ps_ref, o_ref, ext_ref, *, seq_len):
    tm, sg_width = zu_ref.shape
    pool_width = zp_ref.shape[1]
    n_heads = sg_width // HEAD_DIM
    n_groups = pool_width // HEAD_DIM

    for c in range(tm // SG_CHUNK):
        rows = slice(c * SG_CHUNK, (c + 1) * SG_CHUNK)
        for h in range(n_heads):
            cols = slice(h * HEAD_DIM, (h + 1) * HEAD_DIM)
            u = _gelu(zu_ref[rows, cols])
            v = _rms_norm(_gelu(zv_ref[rows, cols]), sgn_ref[:, cols])
            mixed = jnp.dot(ws_ref[h], v.astype(BF16), preferred_element_type=F32) + bs_ref[h]
            o_ref[rows, cols] = (u * mixed).astype(o_ref.dtype)

    tile_pos = (pl.program_id(0) * tm) % seq_len
    is_first = tile_pos == 0
    is_last = tile_pos + tm == seq_len
    ext_ref[0:POOL_HALO, :] = jnp.where(is_first, 0.0, prev_ref[...])
    ext_ref[POOL_HALO:POOL_HALO + tm, :] = zp_ref[...]
    ext_ref[POOL_HALO + tm:, :] = jnp.where(is_last, 0.0, next_ref[...])
    pos = tile_pos + lax.broadcasted_iota(jnp.int32, (tm, HEAD_DIM), 0)
    for g in range(n_groups):
        cols = slice(g * HEAD_DIM, (g + 1) * HEAD_DIM)
        half = POOL_WINDOWS[g] // 2
        total = ext_ref[POOL_HALO - half:POOL_HALO - half + tm, cols]
        for k in range(1 - half, half):
            total = total + ext_ref[POOL_HALO + k:POOL_HALO + k + tm, cols]
        cnt = jnp.minimum(pos + half, seq_len) - jnp.maximum(pos - half, 0)
        d = total / cnt.astype(F32) - zp_ref[:, cols]
        y = jnp.dot(d.astype(BF16), pw_ref[g], preferred_element_type=F32) * ps_ref[:, cols]
        o_ref[:, sg_width + g * HEAD_DIM:sg_width + (g + 1) * HEAD_DIM] = y.astype(o_ref.dtype)


def _attn_kernel(q_ref, k_ref, v_ref, b_ref, o_ref, *, n_rows):
    R, W = NA_ROWS_PER_BLOCK, NA_WIN_ROWS
    n_blocks = n_rows // R

    def one_block(i):
        win_start = jnp.clip(i * R - NA_KH // 2, 0, n_rows - W)
        k_start = pl.multiple_of(win_start * GRID_W, R * GRID_W)
        q_start = pl.multiple_of(i * (R * GRID_W), R * GRID_W)
        kind = jnp.where(i == 0, 0, jnp.where(i == n_blocks - 1, 2, 1))
        q = q_ref[0, pl.ds(q_start, R * GRID_W), :]
        k = k_ref[0, pl.ds(k_start, W * GRID_W), :]
        v = v_ref[0, pl.ds(k_start, W * GRID_W), :]
        s = lax.dot_general(q, k, (((1,), (1,)), ((), ())), preferred_element_type=F32)
        s = s * np.float32(HEAD_DIM ** -0.5 * LOG2_E) + b_ref[0, kind, 0]
        p = jnp.exp2(s - jnp.max(s, axis=-1, keepdims=True))
        denom = jnp.sum(p, axis=-1, keepdims=True)
        o = jnp.dot(p.astype(BF16), v, preferred_element_type=F32) / denom
        o_ref[0, pl.ds(q_start, R * GRID_W), :] = o.astype(o_ref.dtype)

    def body(t, carry):
        for u in range(NA_BLOCKS_PER_ITER):
            one_block(t * NA_BLOCKS_PER_ITER + u)
        return carry

    lax.fori_loop(0, n_blocks // NA_BLOCKS_PER_ITER, body, 0)


def _bias_rows_kernel(cb_ref, o_ref, *, plan):
    low_half = lax.broadcasted_iota(jnp.int32, (GRID_W, 2 * GRID_W), 1) < GRID_W
    for kind, kind_plan in enumerate(plan):
        for rq, slabs in enumerate(kind_plan):
            for jp in range(len(slabs) // 2):
                pair = jnp.where(low_half, cb_ref[0, 0, slabs[2 * jp]], cb_ref[0, 0, slabs[2 * jp + 1]])
                o_ref[0, kind, 0, rq * GRID_W:(rq + 1) * GRID_W, jp * 2 * GRID_W:(jp + 1) * 2 * GRID_W] = pair


def _attn_bias_table(rpb, n_rows):
    R, W = NA_ROWS_PER_BLOCK, NA_WIN_ROWS
    n_dr, n_dc = 2 * NA_KH - 1, 2 * NA_KW - 1
    kh = min(NA_KH, n_rows)
    cq = np.arange(GRID_W)[:, None]
    ck = np.arange(GRID_W)[None, :]
    cs = np.clip(cq - NA_KW // 2, 0, GRID_W - NA_KW)
    col_ok = (ck >= cs) & (ck < cs + NA_KW)
    dc = np.clip(ck - cq + NA_KW - 1, 0, n_dc - 1)
    col_sel = (np.arange(n_dc)[:, None, None] == dc[None]) & col_ok[None]
    col_bias = jnp.einsum('lhdc,cqk->lhdqk', rpb, col_sel.astype(np.float32),
                          precision=lax.Precision.HIGHEST)
    col_bias = jnp.where(col_ok, col_bias, NEG)
    neg_slab = jnp.full(col_bias.shape[:2] + (1,) + col_bias.shape[3:], NEG, F32)
    col_bias = jnp.concatenate([col_bias, neg_slab], axis=2)
    col_bias = jnp.concatenate([col_bias, col_bias], axis=-1)
    plan = []
    for r0 in (0, R, n_rows - R):
        win_start = int(np.clip(r0 - NA_KH // 2, 0, n_rows - W))
        kind_plan = []
        for rq in range(R):
            r = r0 + rq
            sr = int(np.clip(r - kh // 2, 0, n_rows - kh))
            rows = [win_start + rk for rk in range(W)]
            kind_plan.append(tuple(row - r + NA_KH - 1 if sr <= row < sr + kh else n_dr for row in rows))
        plan.append(tuple(kind_plan))
    L, H = rpb.shape[:2]
    return pl.pallas_call(
        functools.partial(_bias_rows_kernel, plan=tuple(plan)),
        grid=(L, H),
        in_specs=[pl.BlockSpec((1, 1, n_dr + 1, GRID_W, 2 * GRID_W), lambda l, h: (l, h, 0, 0, 0))],
        out_specs=pl.BlockSpec((1, 3, 1, R * GRID_W, W * GRID_W), lambda l, h: (l, 0, h, 0, 0)),
        out_shape=jax.ShapeDtypeStruct((L, 3, H, R * GRID_W, W * GRID_W), F32),
        compiler_params=pltpu.CompilerParams(dimension_semantics=("parallel", "parallel")),
        name="attn_bias_rows",
    )(col_bias)


def _attn(qkv, bias, layer, *, n_heads, seq_len):
    B = qkv.shape[0]
    n_rows = seq_len // GRID_W
    R, W = NA_ROWS_PER_BLOCK, NA_WIN_ROWS
    assert n_rows >= W and n_rows % (R * NA_BLOCKS_PER_ITER) == 0
    return pl.pallas_call(
        functools.partial(_attn_kernel, n_rows=n_rows),
        grid=(B, n_heads),
        in_specs=[
            pl.BlockSpec((1, seq_len, HEAD_DIM), lambda b, h: (b, 0, h)),
            pl.BlockSpec((1, seq_len, HEAD_DIM), lambda b, h: (b, 0, n_heads + h)),
            pl.BlockSpec((1, seq_len, HEAD_DIM), lambda b, h: (b, 0, 2 * n_heads + h)),
            pl.BlockSpec((1, 3, 1, R * GRID_W, W * GRID_W), lambda b, h: (layer, 0, h, 0, 0)),
        ],
        out_specs=pl.BlockSpec((1, seq_len, HEAD_DIM), lambda b, h: (b, 0, h)),
        out_shape=jax.ShapeDtypeStruct((B, seq_len, n_heads * HEAD_DIM), BF16),
        compiler_params=pltpu.CompilerParams(
            dimension_semantics=("parallel", "parallel"),
            vmem_limit_bytes=V7X_VMEM_LIMIT_BYTES),
        name="nbr_attn",
    )(qkv, qkv, qkv, bias)


def _mix_out_kernel(x_ref, zu_ref, zv_ref, zp_ref, prev_ref, next_ref, c_ref, sgn_ref, ws_ref, bs_ref,
                    pw_ref, ps_ref, w1_ref, w2_ref, o_ref, ext_ref, ab_ref, *, seq_len):
    y_c = jnp.dot(c_ref[...], w2_ref[...], preferred_element_type=F32)
    _gate_pool_tile(zu_ref, zv_ref, zp_ref, prev_ref, next_ref, sgn_ref, ws_ref, bs_ref,
                    pw_ref, ps_ref, ab_ref, ext_ref, seq_len=seq_len)
    y_ab = jnp.dot(ab_ref[...], w1_ref[...], preferred_element_type=F32)
    o_ref[...] = x_ref[...] + (y_ab + y_c)


def _mix_out(x, uvp, c, sgn, ws, bs, pw, ps, w, *, seq_len, sg_width, pool_width, tm):
    T, D = x.shape
    K1, K2 = sg_width + pool_width, c.shape[1]
    assert sg_width == pool_width and seq_len % tm == 0 and tm % SG_CHUNK == 0
    assert K1 == K2 and w.shape[0] == K1 + K2
    halo_blocks = tm // POOL_HALO
    n_heads = sg_width // HEAD_DIM
    n_groups = pool_width // HEAD_DIM
    return pl.pallas_call(
        functools.partial(_mix_out_kernel, seq_len=seq_len),
        grid=(T // tm,),
        in_specs=[
            pl.BlockSpec((tm, D), lambda i: (i, 0)),
            pl.BlockSpec((tm, sg_width), lambda i: (i, 0)),
            pl.BlockSpec((tm, sg_width), lambda i: (i, 1)),
            pl.BlockSpec((tm, pool_width), lambda i: (i, 2)),
            pl.BlockSpec((POOL_HALO, pool_width), lambda i: (jnp.maximum(i * halo_blocks - 1, 0), 2)),
            pl.BlockSpec((POOL_HALO, pool_width),
                         lambda i: (jnp.minimum((i + 1) * halo_blocks, T // POOL_HALO - 1), 2)),
            pl.BlockSpec((tm, K2), lambda i: (i, 0)),
            pl.BlockSpec((1, sg_width), lambda i: (0, 0)),
            pl.BlockSpec((n_heads, SG_CHUNK, SG_CHUNK), lambda i: (0, 0, 0)),
            pl.BlockSpec((n_heads, SG_CHUNK, HEAD_DIM), lambda i: (0, 0, 0)),
            pl.BlockSpec((n_groups, HEAD_DIM, HEAD_DIM), lambda i: (0, 0, 0)),
            pl.BlockSpec((1, pool_width), lambda i: (0, 0)),
            pl.BlockSpec((K1, D), lambda i: (0, 0)),
            pl.BlockSpec((K2, D), lambda i: (1, 0)),
        ],
        out_specs=pl.BlockSpec((tm, D), lambda i: (i, 0)),
        out_shape=jax.ShapeDtypeStruct((T, D), F32),
        scratch_shapes=[pltpu.VMEM((tm + 2 * POOL_HALO, pool_width), F32),
                        pltpu.VMEM((tm, K1), BF16)],
        compiler_params=pltpu.CompilerParams(
            dimension_semantics=("parallel",),
            vmem_limit_bytes=V7X_VMEM_LIMIT_BYTES),
        name="mix_out",
    )(x, uvp, uvp, uvp, uvp, uvp, c, sgn, ws, bs, pw, ps, w, w)


def _pick_tile(n, target):
    t = min(n, target)
    while n % t:
        t //= 2
    return t


def kernel(x, ffn1_norm, ffn1_w_gate, ffn1_w_up, ffn1_w_down, mix_norm, w_in, sg_norm, sg_w, sg_b, pool_w, pool_scale, na_rpb, w_out, ffn2_norm, ffn2_w_gate, ffn2_w_up, ffn2_w_down, final_norm):
    B, S, D = x.shape
    depth = ffn1_w_gate.shape[0]
    sg_width = sg_norm.shape[1]
    pool_width = pool_scale.shape[1]
    n_attn_heads = na_rpb.shape[1]
    T = B * S
    tm_ffn = _pick_tile(T, 1024)
    tf = _pick_tile(ffn1_w_gate.shape[2], 256)
    tm_mix = _pick_tile(S, 512)

    bf = lambda w: w.astype(BF16)
    win, wout, sgw, pw = bf(w_in), bf(w_out), bf(sg_w), bf(pool_w)
    sgb = jnp.broadcast_to(sg_b[..., None], sg_b.shape + (HEAD_DIM,))
    gf = final_norm.reshape(1, D)
    bias = _attn_bias_table(na_rpb * np.float32(LOG2_E), S // GRID_W)

    h = x.reshape(T, D)
    for l in range(depth):
        h = _ffn(h, ffn1_norm[l].reshape(1, D), ffn1_w_gate, ffn1_w_up, ffn1_w_down, gf, l,
                 final_norm=False, tm=tm_ffn, tf=tf)
        uvp, qkv = _inproj(h, mix_norm[l].reshape(1, D), win, l,
                           n_f32_cols=2 * sg_width + pool_width, tm=tm_mix, tn=512)
        c = _attn(qkv.reshape(B, S, -1), bias, l, n_heads=n_attn_heads, seq_len=S)
        h = _mix_out(h, uvp, c.reshape(T, -1), sg_norm[l].reshape(1, sg_width), sgw[l], sgb[l], pw[l],
                     pool_scale[l].reshape(1, pool_width), wout[l],
                     seq_len=S, sg_width=sg_width, pool_width=pool_width, tm=tm_mix)
        h = _ffn(h, ffn2_norm[l].reshape(1, D), ffn2_w_gate, ffn2_w_up, ffn2_w_down, gf, l,
                 final_norm=(l == depth - 1), tm=tm_ffn, tf=tf)
    return h.reshape(B, S, D)
```

```python
import functools

import jax
import jax.numpy as jnp
import numpy as np
from jax import lax
from jax.experimental import pallas as pl
from jax.experimental.pallas import tpu as pltpu

EPS = 1e-6
NEG = -1e30
LOG2_E = 1.4426950408889634
HEAD_DIM = 128
SG_CHUNK = 128
POOL_WINDOWS = (2, 4, 8, 16)
POOL_HALO = 8
NA_KH = 8
NA_KW = 16
GRID_W = 64
NA_ROWS_PER_BLOCK = 4
NA_WIN_ROWS = 12
NA_BLOCKS_PER_ITER = 8
EDGE_ROW_SPLIT = 2

V7X_VMEM_LIMIT_BYTES = 58 * 1024 * 1024

F32 = jnp.float32
BF16 = jnp.bfloat16


def _rms_norm(x, g):
    return x * lax.rsqrt(jnp.mean(x * x, axis=-1, keepdims=True) + EPS) * g


def _gelu(x):
    return 0.5 * x * (1.0 + lax.erf(x * np.float32(np.sqrt(0.5))))


def _ffn_kernel(x_ref, g_ref, wg_ref, wu_ref, wd_ref, gf_ref, *rest, final_norm, emit_bf16_weights):
    if emit_bf16_weights:
        o_ref, wg16_ref, wu16_ref, wd16_ref, h_ref = rest
        for src, dst in ((wg_ref, wg16_ref), (wu_ref, wu16_ref), (wd_ref, wd16_ref)):
            dst[...] = src[...].astype(BF16)
        wg_ref, wu_ref, wd_ref = wg16_ref, wu16_ref, wd16_ref
    else:
        _, o_ref, h_ref = rest
    j = pl.program_id(1)
    last = pl.num_programs(1) - 1
    sub = x_ref.shape[0] // EDGE_ROW_SPLIT

    def partial_ffn(h):
        wg, wu, wd = wg_ref[...], wu_ref[...], wd_ref[...]
        gate = jnp.dot(h, wg, preferred_element_type=F32)
        up = jnp.dot(h, wu, preferred_element_type=F32)
        act = (jax.nn.silu(gate) * up).astype(BF16)
        return jnp.dot(act, wd, preferred_element_type=F32)

    @pl.when(j == 0)
    def _():
        for r in range(EDGE_ROW_SPLIT):
            rows = slice(r * sub, (r + 1) * sub)
            h = _rms_norm(x_ref[rows, :], g_ref[...]).astype(BF16)
            h_ref[rows, :] = h
            o_ref[rows, :] = partial_ffn(h)

    @pl.when(jnp.logical_and(j > 0, j < last))
    def _():
        o_ref[...] += partial_ffn(h_ref[...])

    @pl.when(j == last)
    def _():
        for r in range(EDGE_ROW_SPLIT):
            rows = slice(r * sub, (r + 1) * sub)
            y = x_ref[rows, :] + 0.5 * (o_ref[rows, :] + partial_ffn(h_ref[rows, :]))
            if final_norm:
                y = _rms_norm(y, gf_ref[...])
            o_ref[rows, :] = y


def _ffn(x, g, wg, wu, wd, gf, layer, *, final_norm, tm, tf, tf_rest):
    T, D = x.shape
    F = wg.shape[2]
    assert F // tf >= 2 and F // tf_rest >= 2
    n_tiles = T // tm
    params = pltpu.CompilerParams(dimension_semantics=("parallel", "arbitrary"),
                                  vmem_limit_bytes=V7X_VMEM_LIMIT_BYTES)
    vec_spec = pl.BlockSpec((1, D), lambda i, j: (0, 0))
    y, wg16, wu16, wd16 = pl.pallas_call(
        functools.partial(_ffn_kernel, final_norm=final_norm, emit_bf16_weights=True),
        grid=(1, F // tf),
        in_specs=[
            pl.BlockSpec((tm, D), lambda i, j: (0, 0)),
            vec_spec,
            pl.BlockSpec((None, D, tf), lambda i, j: (layer, 0, j)),
            pl.BlockSpec((None, D, tf), lambda i, j: (layer, 0, j)),
            pl.BlockSpec((None, tf, D), lambda i, j: (layer, j, 0)),
            vec_spec,
        ],
        out_specs=[
            pl.BlockSpec((tm, D), lambda i, j: (0, 0)),
            pl.BlockSpec((D, tf), lambda i, j: (0, j)),
            pl.BlockSpec((D, tf), lambda i, j: (0, j)),
            pl.BlockSpec((tf, D), lambda i, j: (j, 0)),
        ],
        out_shape=[
            jax.ShapeDtypeStruct((T, D), F32),
            jax.ShapeDtypeStruct((D, F), BF16),
            jax.ShapeDtypeStruct((D, F), BF16),
            jax.ShapeDtypeStruct((F, D), BF16),
        ],
        scratch_shapes=[pltpu.VMEM((tm, D), BF16)],
        compiler_params=params,
        name="ffn_first_tile",
    )(x, g, wg, wu, wd, gf)
    if n_tiles == 1:
        return y
    return pl.pallas_call(
        functools.partial(_ffn_kernel, final_norm=final_norm, emit_bf16_weights=False),
        grid=(n_tiles - 1, F // tf_rest),
        in_specs=[
            pl.BlockSpec((tm, D), lambda i, j: (i + 1, 0)),
            vec_spec,
            pl.BlockSpec((D, tf_rest), lambda i, j: (0, j)),
            pl.BlockSpec((D, tf_rest), lambda i, j: (0, j)),
            pl.BlockSpec((tf_rest, D), lambda i, j: (j, 0)),
            vec_spec,
            pl.BlockSpec(memory_space=pl.ANY),
        ],
        out_specs=pl.BlockSpec((tm, D), lambda i, j: (i + 1, 0)),
        out_shape=jax.ShapeDtypeStruct((T, D), F32),
        input_output_aliases={6: 0},
        scratch_shapes=[pltpu.VMEM((tm, D), BF16)],
        compiler_params=params,
        name="ffn",
    )(x, g, wg16, wu16, wd16, gf, y)


def _inproj_kernel(x_ref, g_ref, w_ref, uvp_ref, qkv_ref, h_ref, *, tn):
    sub = x_ref.shape[0] // EDGE_ROW_SPLIT
    n_f32 = uvp_ref.shape[1]

    def project(h, col):
        return jnp.dot(h, w_ref[:, col:col + tn], preferred_element_type=F32)

    def store(rows, col, z):
        if col < n_f32:
            uvp_ref[rows, col:col + tn] = z
        else:
            qkv_ref[rows, col - n_f32:col - n_f32 + tn] = z.astype(BF16)

    for r in range(EDGE_ROW_SPLIT):
        rows = slice(r * sub, (r + 1) * sub)
        h = _rms_norm(x_ref[rows, :], g_ref[...]).astype(BF16)
        h_ref[rows, :] = h
        store(rows, 0, project(h, 0))
    for col in range(tn, w_ref.shape[1], tn):
        store(slice(None), col, project(h_ref[...], col))


def _inproj(x, g, w, layer, *, n_f32_cols, tm, tn):
    T, D = x.shape
    N = w.shape[2]
    assert n_f32_cols % tn == 0 and N % tn == 0
    return pl.pallas_call(
        functools.partial(_inproj_kernel, tn=tn),
        grid=(T // tm,),
        in_specs=[
            pl.BlockSpec((tm, D), lambda i: (i, 0)),
            pl.BlockSpec((1, D), lambda i: (0, 0)),
            pl.BlockSpec((None, D, N), lambda i: (layer, 0, 0), pipeline_mode=pl.Buffered(1)),
        ],
        out_specs=[
            pl.BlockSpec((tm, n_f32_cols), lambda i: (i, 0)),
            pl.BlockSpec((tm, N - n_f32_cols), lambda i: (i, 0)),
        ],
        out_shape=[
            jax.ShapeDtypeStruct((T, n_f32_cols), F32),
            jax.ShapeDtypeStruct((T, N - n_f32_cols), BF16),
        ],
        scratch_shapes=[pltpu.VMEM((tm, D), BF16)],
        compiler_params=pltpu.CompilerParams(
            dimension_semantics=("parallel",),
            vmem_limit_bytes=V7X_VMEM_LIMIT_BYTES),
        name="inproj",
    )(x, g, w)


def _gate_pool_tile(zu_ref, zv_ref, zp_ref, prev_ref, next_ref, sgn_ref, ws_ref, bs_ref,
                    pw_ref, ps_ref, o_ref, ext_ref, *, seq_len):
    tm, sg_width = zu_ref.shape
    pool_width = zp_ref.shape[1]
    n_heads = sg_width // HEAD_DIM
    n_groups = pool_width // HEAD_DIM

    for c in range(tm // SG_CHUNK):
        rows = slice(c * SG_CHUNK, (c + 1) * SG_CHUNK)
        for h in range(n_heads):
            cols = slice(h * HEAD_DIM, (h + 1) * HEAD_DIM)
            u = _gelu(zu_ref[rows, cols])
            v = _rms_norm(_gelu(zv_ref[rows, cols]), sgn_ref[:, cols])
            mixed = jnp.dot(ws_ref[h], v.astype(BF16), preferred_element_type=F32) + bs_ref[h]
            o_ref[rows, cols] = (u * mixed).astype(o_ref.dtype)

    tile_pos = (pl.program_id(0) * tm) % seq_len
    is_first = tile_pos == 0
    is_last = tile_pos + tm == seq_len
    ext_ref[0:POOL_HALO, :] = jnp.where(is_first, 0.0, prev_ref[...])
    ext_ref[POOL_HALO:POOL_HALO + tm, :] = zp_ref[...]
    ext_ref[POOL_HALO + tm:, :] = jnp.where(is_last, 0.0, next_ref[...])
    pos = tile_pos + lax.broadcasted_iota(jnp.int32, (tm, HEAD_DIM), 0)
    for g in range(n_groups):
        cols = slice(g * HEAD_DIM, (g + 1) * HEAD_DIM)
        half = POOL_WINDOWS[g] // 2
        total = ext_ref[POOL_HALO - half:POOL_HALO - half + tm, cols]
        for k in range(1 - half, half):
            total = total + ext_ref[POOL_HALO + k:POOL_HALO + k + tm, cols]
        cnt = jnp.minimum(pos + half, seq_len) - jnp.maximum(pos - half, 0)
        d = total / cnt.astype(F32) - zp_ref[:, cols]
        y = jnp.dot(d.astype(BF16), pw_ref[g], preferred_element_type=F32) * ps_ref[:, cols]
        o_ref[:, sg_width + g * HEAD_DIM:sg_width + (g + 1) * HEAD_DIM] = y.astype(o_ref.dtype)


def _attn_kernel(q_ref, k_ref, v_ref, b_ref, o_ref, *, n_rows):
    R, W = NA_ROWS_PER_BLOCK, NA_WIN_ROWS
    n_blocks = n_rows // R

    def one_block(i):
        win_start = jnp.clip(i * R - NA_KH // 2, 0, n_rows - W)
        k_start = pl.multiple_of(win_start * GRID_W, R * GRID_W)
        q_start = pl.multiple_of(i * (R * GRID_W), R * GRID_W)
        kind = jnp.where(i == 0, 0, jnp.where(i == n_blocks - 1, 2, 1))
        q = q_ref[0, pl.ds(q_start, R * GRID_W), :]
        k = k_ref[0, pl.ds(k_start, W * GRID_W), :]
        v = v_ref[0, pl.ds(k_start, W * GRID_W), :]
        s = lax.dot_general(q, k, (((1,), (1,)), ((), ())), preferred_element_type=F32)
        s = s * np.float32(HEAD_DIM ** -0.5 * LOG2_E) + b_ref[0, kind, 0]
        p = jnp.exp2(s - jnp.max(s, axis=-1, keepdims=True))
        denom = jnp.sum(p, axis=-1, keepdims=True)
        o = jnp.dot(p.astype(BF16), v, preferred_element_type=F32) / denom
        o_ref[0, pl.ds(q_start, R * GRID_W), :] = o.astype(o_ref.dtype)

    def body(t, carry):
        for u in range(NA_BLOCKS_PER_ITER):
            one_block(t * NA_BLOCKS_PER_ITER + u)
        return carry

    lax.fori_loop(0, n_blocks // NA_BLOCKS_PER_ITER, body, 0)


def _bias_rows_kernel(cb_ref, o_ref, *, plan):
    low_half = lax.broadcasted_iota(jnp.int32, (GRID_W, 2 * GRID_W), 1) < GRID_W
    for kind, kind_plan in enumerate(plan):
        for rq, slabs in enumerate(kind_plan):
            for jp in range(len(slabs) // 2):
                pair = jnp.where(low_half, cb_ref[0, 0, slabs[2 * jp]], cb_ref[0, 0, slabs[2 * jp + 1]])
                o_ref[0, kind, 0, rq * GRID_W:(rq + 1) * GRID_W, jp * 2 * GRID_W:(jp + 1) * 2 * GRID_W] = pair


def _attn_bias_table(rpb, n_rows):
    R, W = NA_ROWS_PER_BLOCK, NA_WIN_ROWS
    n_dr, n_dc = 2 * NA_KH - 1, 2 * NA_KW - 1
    kh = min(NA_KH, n_rows)
    cq = np.arange(GRID_W)[:, None]
    ck = np.arange(GRID_W)[None, :]
    cs = np.clip(cq - NA_KW // 2, 0, GRID_W - NA_KW)
    col_ok = (ck >= cs) & (ck < cs + NA_KW)
    dc = np.clip(ck - cq + NA_KW - 1, 0, n_dc - 1)
    col_sel = (np.arange(n_dc)[:, None, None] == dc[None]) & col_ok[None]
    col_bias = jnp.einsum('lhdc,cqk->lhdqk', rpb, col_sel.astype(np.float32),
                          precision=lax.Precision.HIGHEST)
    col_bias = jnp.where(col_ok, col_bias, NEG)
    neg_slab = jnp.full(col_bias.shape[:2] + (1,) + col_bias.shape[3:], NEG, F32)
    col_bias = jnp.concatenate([col_bias, neg_slab], axis=2)
    col_bias = jnp.concatenate([col_bias, col_bias], axis=-1)
    plan = []
    for r0 in (0, R, n_rows - R):
        win_start = int(np.clip(r0 - NA_KH // 2, 0, n_rows - W))
        kind_plan = []
        for rq in range(R):
            r = r0 + rq
            sr = int(np.clip(r - kh // 2, 0, n_rows - kh))
            rows = [win_start + rk for rk in range(W)]
            kind_plan.append(tuple(row - r + NA_KH - 1 if sr <= row < sr + kh else n_dr for row in rows))
        plan.append(tuple(kind_plan))
    L, H = rpb.shape[:2]
    return pl.pallas_call(
        functools.partial(_bias_rows_kernel, plan=tuple(plan)),
        grid=(L, H),
        in_specs=[pl.BlockSpec((1, 1, n_dr + 1, GRID_W, 2 * GRID_W), lambda l, h: (l, h, 0, 0, 0))],
        out_specs=pl.BlockSpec((1, 3, 1, R * GRID_W, W * GRID_W), lambda l, h: (l, 0, h, 0, 0)),
        out_shape=jax.ShapeDtypeStruct((L, 3, H, R * GRID_W, W * GRID_W), F32),
        compiler_params=pltpu.CompilerParams(dimension_semantics=("parallel", "parallel")),
        name="attn_bias_rows",
    )(col_bias)


def _attn(qkv, bias, layer, *, n_heads, seq_len):
    B = qkv.shape[0]
    n_rows = seq_len // GRID_W
    R, W = NA_ROWS_PER_BLOCK, NA_WIN_ROWS
    assert n_rows >= W and n_rows % (R * NA_BLOCKS_PER_ITER) == 0
    return pl.pallas_call(
        functools.partial(_attn_kernel, n_rows=n_rows),
        grid=(B, n_heads),
        in_specs=[
            pl.BlockSpec((1, seq_len, HEAD_DIM), lambda b, h: (b, 0, h)),
            pl.BlockSpec((1, seq_len, HEAD_DIM), lambda b, h: (b, 0, n_heads + h)),
            pl.BlockSpec((1, seq_len, HEAD_DIM), lambda b, h: (b, 0, 2 * n_heads + h)),
            pl.BlockSpec((1, 3, 1, R * GRID_W, W * GRID_W), lambda b, h: (layer, 0, h, 0, 0)),
        ],
        out_specs=pl.BlockSpec((1, seq_len, HEAD_DIM), lambda b, h: (b, 0, h)),
        out_shape=jax.ShapeDtypeStruct((B, seq_len, n_heads * HEAD_DIM), BF16),
        compiler_params=pltpu.CompilerParams(
            dimension_semantics=("parallel", "parallel"),
            vmem_limit_bytes=V7X_VMEM_LIMIT_BYTES),
        name="nbr_attn",
    )(qkv, qkv, qkv, bias)


def _mix_out_kernel(x_ref, zu_ref, zv_ref, zp_ref, prev_ref, next_ref, c_ref, sgn_ref, ws_ref, bs_ref,
                    pw_ref, ps_ref, w1_ref, w2_ref, o_ref, ext_ref, ab_ref, *, seq_len):
    y_c = jnp.dot(c_ref[...], w2_ref[...], preferred_element_type=F32)
    _gate_pool_tile(zu_ref, zv_ref, zp_ref, prev_ref, next_ref, sgn_ref, ws_ref, bs_ref,
                    pw_ref, ps_ref, ab_ref, ext_ref, seq_len=seq_len)
    y_ab = jnp.dot(ab_ref[...], w1_ref[...], preferred_element_type=F32)
    o_ref[...] = x_ref[...] + (y_ab + y_c)


def _mix_out(x, uvp, c, sgn, ws, bs, pw, ps, w, *, seq_len, sg_width, pool_width, tm):
    T, D = x.shape
    K1, K2 = sg_width + pool_width, c.shape[1]
    assert sg_width == pool_width and seq_len % tm == 0 and tm % SG_CHUNK == 0
    assert K1 == K2 and w.shape[0] == K1 + K2
    halo_blocks = tm // POOL_HALO
    n_heads = sg_width // HEAD_DIM
    n_groups = pool_width // HEAD_DIM
    return pl.pallas_call(
        functools.partial(_mix_out_kernel, seq_len=seq_len),
        grid=(T // tm,),
        in_specs=[
            pl.BlockSpec((tm, D), lambda i: (i, 0)),
            pl.BlockSpec((tm, sg_width), lambda i: (i, 0)),
            pl.BlockSpec((tm, sg_width), lambda i: (i, 1)),
            pl.BlockSpec((tm, pool_width), lambda i: (i, 2)),
            pl.BlockSpec((POOL_HALO, pool_width), lambda i: (jnp.maximum(i * halo_blocks - 1, 0), 2)),
            pl.BlockSpec((POOL_HALO, pool_width),
                         lambda i: (jnp.minimum((i + 1) * halo_blocks, T // POOL_HALO - 1), 2)),
            pl.BlockSpec((tm, K2), lambda i: (i, 0)),
            pl.BlockSpec((1, sg_width), lambda i: (0, 0)),
            pl.BlockSpec((n_heads, SG_CHUNK, SG_CHUNK), lambda i: (0, 0, 0)),
            pl.BlockSpec((n_heads, SG_CHUNK, HEAD_DIM), lambda i: (0, 0, 0)),
            pl.BlockSpec((n_groups, HEAD_DIM, HEAD_DIM), lambda i: (0, 0, 0)),
            pl.BlockSpec((1, pool_width), lambda i: (0, 0)),
            pl.BlockSpec((K1, D), lambda i: (0, 0)),
            pl.BlockSpec((K2, D), lambda i: (1, 0)),
        ],
        out_specs=pl.BlockSpec((tm, D), lambda i: (i, 0)),
        out_shape=jax.ShapeDtypeStruct((T, D), F32),
        scratch_shapes=[pltpu.VMEM((tm + 2 * POOL_HALO, pool_width), F32),
                        pltpu.VMEM((tm, K1), BF16)],
        compiler_params=pltpu.CompilerParams(
            dimension_semantics=("parallel",),
            vmem_limit_bytes=V7X_VMEM_LIMIT_BYTES),
        name="mix_out",
    )(x, uvp, uvp, uvp, uvp, uvp, c, sgn, ws, bs, pw, ps, w, w)


def _pick_tile(n, target):
    t = min(n, target)
    while n % t:
        t //= 2
    return t


def kernel(x, ffn1_norm, ffn1_w_gate, ffn1_w_up, ffn1_w_down, mix_norm, w_in, sg_norm, sg_w, sg_b, pool_w, pool_scale, na_rpb, w_out, ffn2_norm, ffn2_w_gate, ffn2_w_up, ffn2_w_down, final_norm):
    B, S, D = x.shape
    depth = ffn1_w_gate.shape[0]
    sg_width = sg_norm.shape[1]
    pool_width = pool_scale.shape[1]
    n_attn_heads = na_rpb.shape[1]
    T = B * S
    tm_ffn = _pick_tile(T, 1024)
    tf = _pick_tile(ffn1_w_gate.shape[2], 256)
    tf_rest = _pick_tile(ffn1_w_gate.shape[2], 512)
    tm_mix = _pick_tile(S, 512)

    bf = lambda w: w.astype(BF16)
    win, wout, sgw, pw = bf(w_in), bf(w_out), bf(sg_w), bf(pool_w)
    sgb = jnp.broadcast_to(sg_b[..., None], sg_b.shape + (HEAD_DIM,))
    gf = final_norm.reshape(1, D)
    bias = _attn_bias_table(na_rpb * np.float32(LOG2_E), S // GRID_W)

    h = x.reshape(T, D)
    for l in range(depth):
        h = _ffn(h, ffn1_norm[l].reshape(1, D), ffn1_w_gate, ffn1_w_up, ffn1_w_down, gf, l,
                 final_norm=False, tm=tm_ffn, tf=tf, tf_rest=tf_rest)
        uvp, qkv = _inproj(h, mix_norm[l].reshape(1, D), win, l,
                           n_f32_cols=2 * sg_width + pool_width, tm=tm_mix, tn=512)
        c = _attn(qkv.reshape(B, S, -1), bias, l, n_heads=n_attn_heads, seq_len=S)
        h = _mix_out(h, uvp, c.reshape(T, -1), sg_norm[l].reshape(1, sg_width), sgw[l], sgb[l], pw[l],
                     pool_scale[l].reshape(1, pool_width), wout[l],
                     seq_len=S, sg_width=sg_width, pool_width=pool_width, tm=tm_mix)
        h = _ffn(h, ffn2_norm[l].reshape(1, D), ffn2_w_gate, ffn2_w_up, ffn2_w_down, gf, l,
                 final_norm=(l == depth - 1), tm=tm_ffn, tf=tf, tf_rest=tf_rest)
    return h.reshape(B, S, D)
```

```python
import functools

import jax
import jax.numpy as jnp
import numpy as np
from jax import lax
from jax.experimental import pallas as pl
from jax.experimental.pallas import tpu as pltpu

EPS = 1e-6
NEG = -1e30
LOG2_E = 1.4426950408889634
HEAD_DIM = 128
SG_CHUNK = 128
POOL_WINDOWS = (2, 4, 8, 16)
POOL_HALO = 8
NA_KH = 8
NA_KW = 16
GRID_W = 64
NA_ROWS_PER_BLOCK = 4
NA_WIN_ROWS = 12
NA_BLOCKS_PER_ITER = 8
EDGE_ROW_SPLIT = 2

V7X_VMEM_LIMIT_BYTES = 58 * 1024 * 1024

F32 = jnp.float32
BF16 = jnp.bfloat16


def _rms_norm(x, g):
    return x * lax.rsqrt(jnp.mean(x * x, axis=-1, keepdims=True) + EPS) * g


def _gelu(x):
    return 0.5 * x * (1.0 + lax.erf(x * np.float32(np.sqrt(0.5))))


def _ffn_kernel(x_ref, g_ref, wg_ref, wu_ref, wd_ref, gf_ref, *rest, final_norm, emit_bf16_weights):
    if emit_bf16_weights:
        o_ref, wg16_ref, wu16_ref, wd16_ref, h_ref = rest
        for src, dst in ((wg_ref, wg16_ref), (wu_ref, wu16_ref), (wd_ref, wd16_ref)):
            dst[...] = src[...].astype(BF16)
        wg_ref, wu_ref, wd_ref = wg16_ref, wu16_ref, wd16_ref
    else:
        o_ref, h_ref = rest
    j = pl.program_id(1)
    last = pl.num_programs(1) - 1
    sub = x_ref.shape[0] // EDGE_ROW_SPLIT

    def partial_ffn(h):
        wg, wu, wd = wg_ref[...], wu_ref[...], wd_ref[...]
        gate = jnp.dot(h, wg, preferred_element_type=F32)
        up = jnp.dot(h, wu, preferred_element_type=F32)
        act = (jax.nn.silu(gate) * up).astype(BF16)
        return jnp.dot(act, wd, preferred_element_type=F32)

    @pl.when(j == 0)
    def _():
        for r in range(EDGE_ROW_SPLIT):
            rows = slice(r * sub, (r + 1) * sub)
            h = _rms_norm(x_ref[rows, :], g_ref[...]).astype(BF16)
            h_ref[rows, :] = h
            o_ref[rows, :] = partial_ffn(h)

    @pl.when(jnp.logical_and(j > 0, j < last))
    def _():
        o_ref[...] += partial_ffn(h_ref[...])

    @pl.when(j == last)
    def _():
        for r in range(EDGE_ROW_SPLIT):
            rows = slice(r * sub, (r + 1) * sub)
            y = x_ref[rows, :] + 0.5 * (o_ref[rows, :] + partial_ffn(h_ref[rows, :]))
            if final_norm:
                y = _rms_norm(y, gf_ref[...])
            o_ref[rows, :] = y


def _ffn(x, g, wg, wu, wd, gf, layer, *, final_norm, tm, tf, tf_rest):
    T, D = x.shape
    F = wg.shape[2]
    assert F // tf >= 2 and F // tf_rest >= 2
    n_tiles = T // tm
    params = pltpu.CompilerParams(dimension_semantics=("parallel", "arbitrary"),
                                  vmem_limit_bytes=V7X_VMEM_LIMIT_BYTES)
    vec_spec = pl.BlockSpec((1, D), lambda i, j: (0, 0))
    y, wg16, wu16, wd16 = pl.pallas_call(
        functools.partial(_ffn_kernel, final_norm=final_norm, emit_bf16_weights=True),
        grid=(1, F // tf),
        in_specs=[
            pl.BlockSpec((tm, D), lambda i, j: (0, 0)),
            vec_spec,
            pl.BlockSpec((None, D, tf), lambda i, j: (layer, 0, j)),
            pl.BlockSpec((None, D, tf), lambda i, j: (layer, 0, j)),
            pl.BlockSpec((None, tf, D), lambda i, j: (layer, j, 0)),
            vec_spec,
        ],
        out_specs=[
            pl.BlockSpec((tm, D), lambda i, j: (0, 0)),
            pl.BlockSpec((D, tf), lambda i, j: (0, j)),
            pl.BlockSpec((D, tf), lambda i, j: (0, j)),
            pl.BlockSpec((tf, D), lambda i, j: (j, 0)),
        ],
        out_shape=[
            jax.ShapeDtypeStruct((T, D), F32),
            jax.ShapeDtypeStruct((D, F), BF16),
            jax.ShapeDtypeStruct((D, F), BF16),
            jax.ShapeDtypeStruct((F, D), BF16),
        ],
        input_output_aliases={0: 0},
        scratch_shapes=[pltpu.VMEM((tm, D), BF16)],
        compiler_params=params,
        name="ffn_first_tile",
    )(x, g, wg, wu, wd, gf)
    if n_tiles == 1:
        return y
    return pl.pallas_call(
        functools.partial(_ffn_kernel, final_norm=final_norm, emit_bf16_weights=False),
        grid=(n_tiles - 1, F // tf_rest),
        in_specs=[
            pl.BlockSpec((tm, D), lambda i, j: (i + 1, 0)),
            vec_spec,
            pl.BlockSpec((D, tf_rest), lambda i, j: (0, j)),
            pl.BlockSpec((D, tf_rest), lambda i, j: (0, j)),
            pl.BlockSpec((tf_rest, D), lambda i, j: (j, 0)),
            vec_spec,
        ],
        out_specs=pl.BlockSpec((tm, D), lambda i, j: (i + 1, 0)),
        out_shape=jax.ShapeDtypeStruct((T, D), F32),
        input_output_aliases={0: 0},
        scratch_shapes=[pltpu.VMEM((tm, D), BF16)],
        compiler_params=params,
        name="ffn",
    )(y, g, wg16, wu16, wd16, gf)


def _inproj_kernel(x_ref, g_ref, w_ref, uvp_ref, qkv_ref, h_ref, *, tn):
    sub = x_ref.shape[0] // EDGE_ROW_SPLIT
    n_f32 = uvp_ref.shape[1]

    def project(h, col):
        return jnp.dot(h, w_ref[:, col:col + tn], preferred_element_type=F32)

    def store(rows, col, z):
        if col < n_f32:
            uvp_ref[rows, col:col + tn] = z
        else:
            qkv_ref[rows, col - n_f32:col - n_f32 + tn] = z.astype(BF16)

    for r in range(EDGE_ROW_SPLIT):
        rows = slice(r * sub, (r + 1) * sub)
        h = _rms_norm(x_ref[rows, :], g_ref[...]).astype(BF16)
        h_ref[rows, :] = h
        store(rows, 0, project(h, 0))
    for col in range(tn, w_ref.shape[1], tn):
        store(slice(None), col, project(h_ref[...], col))


def _inproj(x, g, w, layer, *, n_f32_cols, tm, tn):
    T, D = x.shape
    N = w.shape[2]
    assert n_f32_cols % tn == 0 and N % tn == 0
    return pl.pallas_call(
        functools.partial(_inproj_kernel, tn=tn),
        grid=(T // tm,),
        in_specs=[
            pl.BlockSpec((tm, D), lambda i: (i, 0)),
            pl.BlockSpec((1, D), lambda i: (0, 0)),
            pl.BlockSpec((None, D, N), lambda i: (layer, 0, 0), pipeline_mode=pl.Buffered(1)),
        ],
        out_specs=[
            pl.BlockSpec((tm, n_f32_cols), lambda i: (i, 0)),
            pl.BlockSpec((tm, N - n_f32_cols), lambda i: (i, 0)),
        ],
        out_shape=[
            jax.ShapeDtypeStruct((T, n_f32_cols), F32),
            jax.ShapeDtypeStruct((T, N - n_f32_cols), BF16),
        ],
        scratch_shapes=[pltpu.VMEM((tm, D), BF16)],
        compiler_params=pltpu.CompilerParams(
            dimension_semantics=("parallel",),
            vmem_limit_bytes=V7X_VMEM_LIMIT_BYTES),
        name="inproj",
    )(x, g, w)


def _gate_pool_tile(zu_ref, zv_ref, zp_ref, prev_ref, next_ref, sgn_ref, ws_ref, bs_ref,
                    pw_ref, ps_ref, o_ref, ext_ref, *, seq_len):
    tm, sg_width = zu_ref.shape
    pool_width = zp_ref.shape[1]
    n_heads = sg_width // HEAD_DIM
    n_groups = pool_width // HEAD_DIM

    for c in range(tm // SG_CHUNK):
        rows = slice(c * SG_CHUNK, (c + 1) * SG_CHUNK)
        for h in range(n_heads):
            cols = slice(h * HEAD_DIM, (h + 1) * HEAD_DIM)
            u = _gelu(zu_ref[rows, cols])
            v = _rms_norm(_gelu(zv_ref[rows, cols]), sgn_ref[:, cols])
            mixed = jnp.dot(ws_ref[h], v.astype(BF16), preferred_element_type=F32) + bs_ref[h]
            o_ref[rows, cols] = (u * mixed).astype(o_ref.dtype)

    tile_pos = (pl.program_id(0) * tm) % seq_len
    is_first = tile_pos == 0
    is_last = tile_pos + tm == seq_len
    ext_ref[0:POOL_HALO, :] = jnp.where(is_first, 0.0, prev_ref[...])
    ext_ref[POOL_HALO:POOL_HALO + tm, :] = zp_ref[...]
    ext_ref[POOL_HALO + tm:, :] = jnp.where(is_last, 0.0, next_ref[...])
    pos = tile_pos + lax.broadcasted_iota(jnp.int32, (tm, HEAD_DIM), 0)
    for g in range(n_groups):
        cols = slice(g * HEAD_DIM, (g + 1) * HEAD_DIM)
        half = POOL_WINDOWS[g] // 2
        total = ext_ref[POOL_HALO - half:POOL_HALO - half + tm, cols]
        for k in range(1 - half, half):
            total = total + ext_ref[POOL_HALO + k:POOL_HALO + k + tm, cols]
        cnt = jnp.minimum(pos + half, seq_len) - jnp.maximum(pos - half, 0)
        d = total / cnt.astype(F32) - zp_ref[:, cols]
        y = jnp.dot(d.astype(BF16), pw_ref[g], preferred_element_type=F32) * ps_ref[:, cols]
        o_ref[:, sg_width + g * HEAD_DIM:sg_width + (g + 1) * HEAD_DIM] = y.astype(o_ref.dtype)


def _attn_kernel(q_ref, k_ref, v_ref, b_ref, o_ref, *, n_rows):
    R, W = NA_ROWS_PER_BLOCK, NA_WIN_ROWS
    n_blocks = n_rows // R

    def one_block(i):
        win_start = jnp.clip(i * R - NA_KH // 2, 0, n_rows - W)
        k_start = pl.multiple_of(win_start * GRID_W, R * GRID_W)
        q_start = pl.multiple_of(i * (R * GRID_W), R * GRID_W)
        kind = jnp.where(i == 0, 0, jnp.where(i == n_blocks - 1, 2, 1))
        q = q_ref[0, pl.ds(q_start, R * GRID_W), :]
        k = k_ref[0, pl.ds(k_start, W * GRID_W), :]
        v = v_ref[0, pl.ds(k_start, W * GRID_W), :]
        s = lax.dot_general(q, k, (((1,), (1,)), ((), ())), preferred_element_type=F32)
        s = s * np.float32(HEAD_DIM ** -0.5 * LOG2_E) + b_ref[0, kind, 0]
        p = jnp.exp2(s - jnp.max(s, axis=-1, keepdims=True))
        denom = jnp.sum(p, axis=-1, keepdims=True)
        o = jnp.dot(p.astype(BF16), v, preferred_element_type=F32) / denom
        o_ref[0, pl.ds(q_start, R * GRID_W), :] = o.astype(o_ref.dtype)

    def body(t, carry):
        for u in range(NA_BLOCKS_PER_ITER):
            one_block(t * NA_BLOCKS_PER_ITER + u)
        return carry

    lax.fori_loop(0, n_blocks // NA_BLOCKS_PER_ITER, body, 0)


def _bias_rows_kernel(cb_ref, o_ref, *, plan):
    low_half = lax.broadcasted_iota(jnp.int32, (GRID_W, 2 * GRID_W), 1) < GRID_W
    for kind, kind_plan in enumerate(plan):
        for rq, slabs in enumerate(kind_plan):
            for jp in range(len(slabs) // 2):
                pair = jnp.where(low_half, cb_ref[0, 0, slabs[2 * jp]], cb_ref[0, 0, slabs[2 * jp + 1]])
                o_ref[0, kind, 0, rq * GRID_W:(rq + 1) * GRID_W, jp * 2 * GRID_W:(jp + 1) * 2 * GRID_W] = pair


def _attn_bias_table(rpb, n_rows):
    R, W = NA_ROWS_PER_BLOCK, NA_WIN_ROWS
    n_dr, n_dc = 2 * NA_KH - 1, 2 * NA_KW - 1
    kh = min(NA_KH, n_rows)
    cq = np.arange(GRID_W)[:, None]
    ck = np.arange(GRID_W)[None, :]
    cs = np.clip(cq - NA_KW // 2, 0, GRID_W - NA_KW)
    col_ok = (ck >= cs) & (ck < cs + NA_KW)
    dc = np.clip(ck - cq + NA_KW - 1, 0, n_dc - 1)
    col_sel = (np.arange(n_dc)[:, None, None] == dc[None]) & col_ok[None]
    col_bias = jnp.einsum('lhdc,cqk->lhdqk', rpb, col_sel.astype(np.float32),
                          precision=lax.Precision.HIGHEST)
    col_bias = jnp.where(col_ok, col_bias, NEG)
    neg_slab = jnp.full(col_bias.shape[:2] + (1,) + col_bias.shape[3:], NEG, F32)
    col_bias = jnp.concatenate([col_bias, neg_slab], axis=2)
    col_bias = jnp.concatenate([col_bias, col_bias], axis=-1)
    plan = []
    for r0 in (0, R, n_rows - R):
        win_start = int(np.clip(r0 - NA_KH // 2, 0, n_rows - W))
        kind_plan = []
        for rq in range(R):
            r = r0 + rq
            sr = int(np.clip(r - kh // 2, 0, n_rows - kh))
            rows = [win_start + rk for rk in range(W)]
            kind_plan.append(tuple(row - r + NA_KH - 1 if sr <= row < sr + kh else n_dr for row in rows))
        plan.append(tuple(kind_plan))
    L, H = rpb.shape[:2]
    return pl.pallas_call(
        functools.partial(_bias_rows_kernel, plan=tuple(plan)),
        grid=(L, H),
        in_specs=[pl.BlockSpec((1, 1, n_dr + 1, GRID_W, 2 * GRID_W), lambda l, h: (l, h, 0, 0, 0))],
        out_specs=pl.BlockSpec((1, 3, 1, R * GRID_W, W * GRID_W), lambda l, h: (l, 0, h, 0, 0)),
        out_shape=jax.ShapeDtypeStruct((L, 3, H, R * GRID_W, W * GRID_W), F32),
        compiler_params=pltpu.CompilerParams(dimension_semantics=("parallel", "parallel")),
        name="attn_bias_rows",
    )(col_bias)


def _attn(qkv, bias, layer, *, n_heads, seq_len):
    B = qkv.shape[0]
    n_rows = seq_len // GRID_W
    R, W = NA_ROWS_PER_BLOCK, NA_WIN_ROWS
    assert n_rows >= W and n_rows % (R * NA_BLOCKS_PER_ITER) == 0
    return pl.pallas_call(
        functools.partial(_attn_kernel, n_rows=n_rows),
        grid=(B, n_heads),
        in_specs=[
            pl.BlockSpec((1, seq_len, HEAD_DIM), lambda b, h: (b, 0, h)),
            pl.BlockSpec((1, seq_len, HEAD_DIM), lambda b, h: (b, 0, n_heads + h)),
            pl.BlockSpec((1, seq_len, HEAD_DIM), lambda b, h: (b, 0, 2 * n_heads + h)),
            pl.BlockSpec((1, 3, 1, R * GRID_W, W * GRID_W), lambda b, h: (layer, 0, h, 0, 0)),
        ],
        out_specs=pl.BlockSpec((1, seq_len, HEAD_DIM), lambda b, h: (b, 0, h)),
        out_shape=jax.ShapeDtypeStruct((B, seq_len, n_heads * HEAD_DIM), BF16),
        compiler_params=pltpu.CompilerParams(
            dimension_semantics=("parallel", "parallel"),
            vmem_limit_bytes=V7X_VMEM_LIMIT_BYTES),
        name="nbr_attn",
    )(qkv, qkv, qkv, bias)


def _mix_out_kernel(x_ref, zu_ref, zv_ref, zp_ref, prev_ref, next_ref, c_ref, sgn_ref, ws_ref, bs_ref,
                    pw_ref, ps_ref, w1_ref, w2_ref, o_ref, ext_ref, ab_ref, *, seq_len):
    y_c = jnp.dot(c_ref[...], w2_ref[...], preferred_element_type=F32)
    _gate_pool_tile(zu_ref, zv_ref, zp_ref, prev_ref, next_ref, sgn_ref, ws_ref, bs_ref,
                    pw_ref, ps_ref, ab_ref, ext_ref, seq_len=seq_len)
    y_ab = jnp.dot(ab_ref[...], w1_ref[...], preferred_element_type=F32)
    o_ref[...] = x_ref[...] + (y_ab + y_c)


def _mix_out(x, uvp, c, sgn, ws, bs, pw, ps, w, *, seq_len, sg_width, pool_width, tm):
    T, D = x.shape
    K1, K2 = sg_width + pool_width, c.shape[1]
    assert sg_width == pool_width and seq_len % tm == 0 and tm % SG_CHUNK == 0
    assert K1 == K2 and w.shape[0] == K1 + K2
    halo_blocks = tm // POOL_HALO
    n_heads = sg_width // HEAD_DIM
    n_groups = pool_width // HEAD_DIM
    return pl.pallas_call(
        functools.partial(_mix_out_kernel, seq_len=seq_len),
        grid=(T // tm,),
        in_specs=[
            pl.BlockSpec((tm, D), lambda i: (i, 0)),
            pl.BlockSpec((tm, sg_width), lambda i: (i, 0)),
            pl.BlockSpec((tm, sg_width), lambda i: (i, 1)),
            pl.BlockSpec((tm, pool_width), lambda i: (i, 2)),
            pl.BlockSpec((POOL_HALO, pool_width), lambda i: (jnp.maximum(i * halo_blocks - 1, 0), 2)),
            pl.BlockSpec((POOL_HALO, pool_width),
                         lambda i: (jnp.minimum((i + 1) * halo_blocks, T // POOL_HALO - 1), 2)),
            pl.BlockSpec((tm, K2), lambda i: (i, 0)),
            pl.BlockSpec((1, sg_width), lambda i: (0, 0)),
            pl.BlockSpec((n_heads, SG_CHUNK, SG_CHUNK), lambda i: (0, 0, 0)),
            pl.BlockSpec((n_heads, SG_CHUNK, HEAD_DIM), lambda i: (0, 0, 0)),
            pl.BlockSpec((n_groups, HEAD_DIM, HEAD_DIM), lambda i: (0, 0, 0)),
            pl.BlockSpec((1, pool_width), lambda i: (0, 0)),
            pl.BlockSpec((K1, D), lambda i: (0, 0)),
            pl.BlockSpec((K2, D), lambda i: (1, 0)),
        ],
        out_specs=pl.BlockSpec((tm, D), lambda i: (i, 0)),
        out_shape=jax.ShapeDtypeStruct((T, D), F32),
        scratch_shapes=[pltpu.VMEM((tm + 2 * POOL_HALO, pool_width), F32),
                        pltpu.VMEM((tm, K1), BF16)],
        compiler_params=pltpu.CompilerParams(
            dimension_semantics=("parallel",),
            vmem_limit_bytes=V7X_VMEM_LIMIT_BYTES),
        name="mix_out",
    )(x, uvp, uvp, uvp, uvp, uvp, c, sgn, ws, bs, pw, ps, w, w)


def _pick_tile(n, target):
    t = min(n, target)
    while n % t:
        t //= 2
    return t


def kernel(x, ffn1_norm, ffn1_w_gate, ffn1_w_up, ffn1_w_down, mix_norm, w_in, sg_norm, sg_w, sg_b, pool_w, pool_scale, na_rpb, w_out, ffn2_norm, ffn2_w_gate, ffn2_w_up, ffn2_w_down, final_norm):
    B, S, D = x.shape
    depth = ffn1_w_gate.shape[0]
    sg_width = sg_norm.shape[1]
    pool_width = pool_scale.shape[1]
    n_attn_heads = na_rpb.shape[1]
    T = B * S
    tm_ffn = _pick_tile(T, 1024)
    tf = _pick_tile(ffn1_w_gate.shape[2], 256)
    tf_rest = _pick_tile(ffn1_w_gate.shape[2], 512)
    tm_mix = _pick_tile(S, 512)

    bf = lambda w: w.astype(BF16)
    win, wout, sgw, pw = bf(w_in), bf(w_out), bf(sg_w), bf(pool_w)
    sgb = jnp.broadcast_to(sg_b[..., None], sg_b.shape + (HEAD_DIM,))
    gf = final_norm.reshape(1, D)
    bias = _attn_bias_table(na_rpb * np.float32(LOG2_E), S // GRID_W)

    h = x.reshape(T, D)
    for l in range(depth):
        h = _ffn(h, ffn1_norm[l].reshape(1, D), ffn1_w_gate, ffn1_w_up, ffn1_w_down, gf, l,
                 final_norm=False, tm=tm_ffn, tf=tf, tf_rest=tf_rest)
        uvp, qkv = _inproj(h, mix_norm[l].reshape(1, D), win, l,
                           n_f32_cols=2 * sg_width + pool_width, tm=tm_mix, tn=512)
        c = _attn(qkv.reshape(B, S, -1), bias, l, n_heads=n_attn_heads, seq_len=S)
        h = _mix_out(h, uvp, c.reshape(T, -1), sg_norm[l].reshape(1, sg_width), sgw[l], sgb[l], pw[l],
                     pool_scale[l].reshape(1, pool_width), wout[l],
                     seq_len=S, sg_width=sg_width, pool_width=pool_width, tm=tm_mix)
        h = _ffn(h, ffn2_norm[l].reshape(1, D), ffn2_w_gate, ffn2_w_up, ffn2_w_down, gf, l,
                 final_norm=(l == depth - 1), tm=tm_ffn, tf=tf, tf_rest=tf_rest)
    return h.reshape(B, S, D)
```
